```python
import math
import jax
import jax.numpy as jnp
from jax import lax
import numpy as np

D_MODEL = 2048
BATCH = 4
SEQ = 2048
DEPTH = 2
DEC_BATCH = 32
DEC_SEQ = 1
PAST_LEN = 8192
PAGE_SIZE = 128

N_MIXERS = 2
N_A_LAYERS = (DEPTH + 1) // 2
N_B_LAYERS = DEPTH // 2
A_HEADS = 8
A_HEAD_DIM = 128
A_MIX = A_HEADS * 2 * A_HEAD_DIM
B_HEADS = 16
B_HEAD_DIM = 128
B_MIX = B_HEADS * B_HEAD_DIM
IDX_HEADS = 16
IDX_DIM = 128
IDX_TOPK_MAX = 256
B_IN = B_MIX + 2 * B_HEAD_DIM + IDX_HEADS * IDX_DIM + IDX_DIM + IDX_HEADS
N_EXPERTS = 64
TOP_K = 8
N_GROUPS = 8
TOPK_GROUPS = 4
D_EXPERT = 512
D_SHARED = 512
ROUTED_SCALE = 2.5
ROPE_THETA = 10000.0
EPS = 1e-6
Q_BLOCK = 128

kernel_name = 'hybrid_diffattn_dsa_moe_adaln_step'


def rmsnorm(x, g):
    xf = x.astype(jnp.float32)
    y = xf * lax.rsqrt(jnp.mean(xf * xf, axis=-1, keepdims=True) + EPS)
    return (y * g.astype(jnp.float32)).astype(x.dtype)


def rope(x, pos):
    d = x.shape[-1]
    inv = jnp.exp(-math.log(ROPE_THETA) * jnp.arange(0, d, 2, dtype=jnp.float32) / d)
    ang = pos.astype(jnp.float32)[:, None] * inv[None, :]
    ang = ang.reshape((ang.shape[0],) + (1,) * (x.ndim - 3) + (d // 2,))
    cos, sin = jnp.cos(ang), jnp.sin(ang)
    xf = x.astype(jnp.float32)
    x1, x2 = xf[..., : d // 2], xf[..., d // 2:]
    return jnp.concatenate([x1 * cos - x2 * sin, x2 * cos + x1 * sin], axis=-1).astype(x.dtype)


def modulation(c, w, b):
    m = jax.nn.silu(c) @ w + b
    return [t[:, None, :] for t in jnp.split(m, 6, axis=-1)]


def diff_project(h, pos, w_in, qn, kn):
    B, T, _ = h.shape
    q, k, v = jnp.split(h @ w_in, 3, axis=-1)
    q = rope(rmsnorm(q.reshape(B, T, A_HEADS, 2, A_HEAD_DIM), qn), pos)
    k = rope(rmsnorm(k.reshape(B, T, A_HEADS, 2, A_HEAD_DIM), kn), pos)
    v = v.reshape(B, T, A_HEADS, 2 * A_HEAD_DIM)
    return q, k, v


def diff_lambda(lq1, lk1, lq2, lk2, lam_init):
    f = jnp.float32
    return (jnp.exp(jnp.sum(lq1.astype(f) * lk1.astype(f)))
            - jnp.exp(jnp.sum(lq2.astype(f) * lk2.astype(f))) + lam_init)


def diff_core(q, k, v, mask, lam, lam_init, subln):
    s = jnp.einsum('bqhcd,bkhcd->bhcqk', q, k).astype(jnp.float32) * (A_HEAD_DIM ** -0.5)
    p = jax.nn.softmax(jnp.where(mask, s, -jnp.inf), axis=-1)
    a = p[:, :, 0] - lam * p[:, :, 1]
    o = jnp.einsum('bhqk,bkhe->bqhe', a.astype(v.dtype), v)
    o = rmsnorm(o, subln) * (1.0 - lam_init)
    return o.reshape(o.shape[0], o.shape[1], -1)


def diff_attn_prompt(q, k, v, lam, lam_init, subln):
    B, S = q.shape[0], q.shape[1]
    nb = S // Q_BLOCK
    qb = jnp.moveaxis(q.reshape((B, nb, Q_BLOCK) + q.shape[2:]), 1, 0)
    kpos = jnp.arange(S)

    def blk(args):
        qi, i = args
        qpos = i * Q_BLOCK + jnp.arange(Q_BLOCK)
        return diff_core(qi, k, v, kpos[None, :] <= qpos[:, None], lam, lam_init, subln)

    o = lax.map(blk, (qb, jnp.arange(nb)))
    return jnp.moveaxis(o, 0, 1).reshape(B, S, -1)


def diff_attn_sample(q, k_new, v_new, cache_k, cache_v, page_table, lam, lam_init, subln):
    Ts = q.shape[1]
    past = page_table.shape[1] * cache_k.shape[1]
    kpos = jnp.arange(past + Ts)
    qpos = past + jnp.arange(Ts)
    mask = kpos[None, :] <= qpos[:, None]

    def one(args):
        qb, kb, vb, pt = args
        kp = cache_k[pt].reshape(past, A_HEADS, 2, A_HEAD_DIM)
        vp = cache_v[pt].reshape(past, A_HEADS, 2 * A_HEAD_DIM)
        kk = jnp.concatenate([kp, kb], axis=0)[None]
        vv = jnp.concatenate([vp, vb], axis=0)[None]
        return diff_core(qb[None], kk, vv, mask, lam, lam_init, subln)[0]

    return lax.map(one, (q, k_new, v_new, page_table))


def dsa_project(h, pos, w_in, qn, kn):
    B, T, _ = h.shape
    sizes = [B_MIX, B_HEAD_DIM, B_HEAD_DIM, IDX_HEADS * IDX_DIM, IDX_DIM]
    q, k, v, qi, ki, wi = jnp.split(h @ w_in, np.cumsum(sizes).tolist(), axis=-1)
    q = rope(rmsnorm(q.reshape(B, T, B_HEADS, B_HEAD_DIM), qn), pos)
    k = rope(rmsnorm(k.reshape(B, T, 1, B_HEAD_DIM), kn), pos)[:, :, 0]
    qi = rope(qi.reshape(B, T, IDX_HEADS, IDX_DIM), pos)
    ki = rope(ki.reshape(B, T, 1, IDX_DIM), pos)[:, :, 0]
    wi = wi * (IDX_HEADS ** -0.5)
    return q, k, v, qi, ki, wi


def index_scores(qi, wi, ki, mask):
    s = jax.nn.relu(jnp.einsum('bqhd,bkd->bqhk', qi, ki).astype(jnp.float32))
    I = jnp.einsum('bqh,bqhk->bqk', wi.astype(jnp.float32), s) * (IDX_DIM ** -0.5)
    return jnp.where(mask, I, -jnp.inf)


def sparse_core(q, ksel, vsel, valid):
    s = jnp.einsum('bqhd,bqkd->bqhk', q, ksel).astype(jnp.float32) * (B_HEAD_DIM ** -0.5)
    p = jax.nn.softmax(jnp.where(valid[:, :, None, :], s, -jnp.inf), axis=-1)
    o = jnp.einsum('bqhk,bqkd->bqhd', p.astype(vsel.dtype), vsel)
    return o.reshape(o.shape[0], o.shape[1], -1)


def dsa_prompt(q, k, v, qi, ki, wi):
    B, S = q.shape[0], q.shape[1]
    n_sel = min(IDX_TOPK_MAX, S // 4)
    nb = S // Q_BLOCK
    kpos = jnp.arange(S)
    bidx = jnp.arange(B)[:, None, None]

    def to_blocks(t):
        return jnp.moveaxis(t.reshape((B, nb, Q_BLOCK) + t.shape[2:]), 1, 0)

    def blk(args):
        qb, qib, wib, i = args
        qpos = i * Q_BLOCK + jnp.arange(Q_BLOCK)
        I = index_scores(qib, wib, ki, kpos[None, :] <= qpos[:, None])
        vals, idx = lax.top_k(I, n_sel)
        return sparse_core(qb, k[bidx, idx], v[bidx, idx], jnp.isfinite(vals))

    o = lax.map(blk, (to_blocks(q), to_blocks(qi), to_blocks(wi), jnp.arange(nb)))
    return jnp.moveaxis(o, 0, 1).reshape(B, S, -1)


def dsa_sample(q, k_new, v_new, qi, ki_new, wi, cache_k, cache_v, cache_idx, page_table):
    Bd, Ts = q.shape[0], q.shape[1]
    page = cache_k.shape[1]
    past = page_table.shape[1] * page
    L = past + Ts
    n_sel = min(IDX_TOPK_MAX, L // 4)
    ki_all = jnp.concatenate([cache_idx[page_table].reshape(Bd, past, IDX_DIM), ki_new], axis=1)
    kpos = jnp.arange(L)
    qpos = past + jnp.arange(Ts)
    I = index_scores(qi, wi, ki_all, kpos[None, :] <= qpos[:, None])
    vals, idx = lax.top_k(I, n_sel)
    bidx = jnp.arange(Bd)[:, None, None]
    in_past = (idx < past)[..., None]
    pidx = jnp.minimum(idx, past - 1)
    phys = page_table[bidx, pidx // page]
    off = pidx % page
    nidx = jnp.clip(idx - past, 0, Ts - 1)
    ksel = jnp.where(in_past, cache_k[phys, off], k_new[bidx, nidx])
    vsel = jnp.where(in_past, cache_v[phys, off], v_new[bidx, nidx])
    return sparse_core(q, ksel, vsel, jnp.isfinite(vals))


def moe(h, w_router, b_router, w_gate, w_up, w_down, ws_gate, ws_up, ws_down):
    f32 = jnp.float32
    shp = h.shape
    x = h.reshape(-1, shp[-1])
    T = x.shape[0]
    scores = jax.nn.sigmoid((x @ w_router).astype(f32))
    biased = scores + b_router.astype(f32)
    grp = biased.reshape(T, N_GROUPS, N_EXPERTS // N_GROUPS)
    grp_score = jnp.sum(lax.top_k(grp, 2)[0], axis=-1)
    _, gidx = lax.top_k(grp_score, TOPK_GROUPS)
    gmask = jnp.sum(jax.nn.one_hot(gidx, N_GROUPS, dtype=f32), axis=-2) > 0
    emask = jnp.repeat(gmask, N_EXPERTS // N_GROUPS, axis=-1)
    _, eidx = lax.top_k(jnp.where(emask, biased, -jnp.inf), TOP_K)
    w = jnp.take_along_axis(scores, eidx, axis=-1)
    w = w / jnp.sum(w, axis=-1, keepdims=True) * ROUTED_SCALE
    gates = jnp.einsum('tk,tke->te', w, jax.nn.one_hot(eidx, N_EXPERTS, dtype=f32))

    def expert(acc, p):
        wg, wu, wd, g = p
        y = (jax.nn.silu(x @ wg) * (x @ wu)) @ wd
        return acc + g[:, None] * y.astype(f32), None

    acc, _ = lax.scan(expert, jnp.zeros((T, shp[-1]), f32), (w_gate, w_up, w_down, gates.T))
    shared = (jax.nn.silu(x @ ws_gate) * (x @ ws_up)) @ ws_down
    return (shared.astype(f32) + acc).astype(h.dtype).reshape(shp)


def setup_inputs(seed: int = 0) -> dict:
    key = jax.random.key(seed)
    ks = iter(jax.random.split(key, 48))
    f = jnp.float32
    D = D_MODEL

    def nrm(shape, scale):
        return jax.random.normal(next(ks), shape, f) * scale

    def gain(shape):
        return 1.0 + nrm(shape, 0.05)

    n_pages = PAST_LEN // PAGE_SIZE
    n_used = DEC_BATCH * n_pages
    pool = n_used + max(1, n_used // 4)
    page_table = jax.random.permutation(next(ks), pool)[:n_used].reshape(DEC_BATCH, n_pages).astype(jnp.int32)
    return {
        'x_prompt': nrm((BATCH, SEQ, D), 1.0),
        'x_sample': nrm((DEC_BATCH, DEC_SEQ, D), 1.0),
        'cache_a_k': nrm((N_A_LAYERS, pool, PAGE_SIZE, A_HEADS, 2 * A_HEAD_DIM), 1.0),
        'cache_a_v': nrm((N_A_LAYERS, pool, PAGE_SIZE, A_HEADS, 2 * A_HEAD_DIM), 1.0),
        'cache_b_k': nrm((N_B_LAYERS, pool, PAGE_SIZE, B_HEAD_DIM), 1.0),
        'cache_b_v': nrm((N_B_LAYERS, pool, PAGE_SIZE, B_HEAD_DIM), 1.0),
        'cache_b_idx': nrm((N_B_LAYERS, pool, PAGE_SIZE, IDX_DIM), 1.0),
        'page_table': page_table,
        'c_prompt': nrm((BATCH, D), 1.0),
        'c_sample': nrm((DEC_BATCH, D), 1.0),
        'w_ada': nrm((DEPTH, D, 6 * D), 0.5 * D ** -0.5),
        'b_ada': nrm((DEPTH, 6 * D), 0.02),
        'g_mix': gain((DEPTH, D)),
        'g_ffn': gain((DEPTH, D)),
        'a_w_in': nrm((N_A_LAYERS, D, 3 * A_MIX), D ** -0.5),
        'a_q_norm': gain((N_A_LAYERS, A_HEAD_DIM)),
        'a_k_norm': gain((N_A_LAYERS, A_HEAD_DIM)),
        'a_lambda_q1': nrm((N_A_LAYERS, A_HEAD_DIM), 0.1),
        'a_lambda_k1': nrm((N_A_LAYERS, A_HEAD_DIM), 0.1),
        'a_lambda_q2': nrm((N_A_LAYERS, A_HEAD_DIM), 0.1),
        'a_lambda_k2': nrm((N_A_LAYERS, A_HEAD_DIM), 0.1),
        'a_subln': gain((N_A_LAYERS, 2 * A_HEAD_DIM)),
        'a_w_out': nrm((N_A_LAYERS, A_MIX, D), A_MIX ** -0.5),
        'b_w_in': nrm((N_B_LAYERS, D, B_IN), D ** -0.5),
        'b_q_norm': gain((N_B_LAYERS, B_HEAD_DIM)),
        'b_k_norm': gain((N_B_LAYERS, B_HEAD_DIM)),
        'b_w_out': nrm((N_B_LAYERS, B_MIX, D), B_MIX ** -0.5),
        'w_router': nrm((DEPTH, D, N_EXPERTS), D ** -0.5),
        'b_router': nrm((DEPTH, N_EXPERTS), 0.01),
        'w_gate': nrm((DEPTH, N_EXPERTS, D, D_EXPERT), D ** -0.5),
        'w_up': nrm((DEPTH, N_EXPERTS, D, D_EXPERT), D ** -0.5),
        'w_down': nrm((DEPTH, N_EXPERTS, D_EXPERT, D), D_EXPERT ** -0.5),
        'ws_gate': nrm((DEPTH, D, D_SHARED), D ** -0.5),
        'ws_up': nrm((DEPTH, D, D_SHARED), D ** -0.5),
        'ws_down': nrm((DEPTH, D_SHARED, D), D_SHARED ** -0.5),
    }


def reference(x_prompt, x_sample, cache_a_k, cache_a_v, cache_b_k, cache_b_v, cache_b_idx,
              page_table, c_prompt, c_sample, w_ada, b_ada, g_mix, g_ffn,
              a_w_in, a_q_norm, a_k_norm, a_lambda_q1, a_lambda_k1, a_lambda_q2, a_lambda_k2,
              a_subln, a_w_out, b_w_in, b_q_norm, b_k_norm, b_w_out,
              w_router, b_router, w_gate, w_up, w_down, ws_gate, ws_up, ws_down):
    S = x_prompt.shape[1]
    Ts = x_sample.shape[1]
    past = page_table.shape[1] * cache_a_k.shape[2]
    pos_p = jnp.arange(S, dtype=jnp.int32)
    pos_s = past + jnp.arange(Ts, dtype=jnp.int32)
    xp, xs = x_prompt, x_sample
    a_kp, a_vp, a_ks, a_vs = [], [], [], []
    b_kp, b_vp, b_ip, b_ks, b_vs, b_is = [], [], [], [], [], []
    for i in range(DEPTH):
        mp = modulation(c_prompt, w_ada[i], b_ada[i])
        ms = modulation(c_sample, w_ada[i], b_ada[i])
        hp = rmsnorm(xp, g_mix[i]) * (1 + mp[1]) + mp[0]
        hs = rmsnorm(xs, g_mix[i]) * (1 + ms[1]) + ms[0]
        j = i // N_MIXERS
        if i % N_MIXERS == 0:
            lam_init = 0.8 - 0.6 * math.exp(-0.3 * i)
            lam = diff_lambda(a_lambda_q1[j], a_lambda_k1[j], a_lambda_q2[j], a_lambda_k2[j], lam_init)
            qp, kp, vp = diff_project(hp, pos_p, a_w_in[j], a_q_norm[j], a_k_norm[j])
            qs, ks, vs = diff_project(hs, pos_s, a_w_in[j], a_q_norm[j], a_k_norm[j])
            op = diff_attn_prompt(qp, kp, vp, lam, lam_init, a_subln[j]) @ a_w_out[j]
            os_ = diff_attn_sample(qs, ks, vs, cache_a_k[j], cache_a_v[j], page_table,
                                   lam, lam_init, a_subln[j]) @ a_w_out[j]
            a_kp.append(kp.reshape(kp.shape[:3] + (-1,)))
            a_vp.append(vp)
            a_ks.append(ks.reshape(ks.shape[:3] + (-1,)))
            a_vs.append(vs)
        else:
            qp, kp, vp, qip, kip, wip = dsa_project(hp, pos_p, b_w_in[j], b_q_norm[j], b_k_norm[j])
            qs, ks, vs, qis, kis, wis = dsa_project(hs, pos_s, b_w_in[j], b_q_norm[j], b_k_norm[j])
            op = dsa_prompt(qp, kp, vp, qip, kip, wip) @ b_w_out[j]
            os_ = dsa_sample(qs, ks, vs, qis, kis, wis, cache_b_k[j], cache_b_v[j],
                             cache_b_idx[j], page_table) @ b_w_out[j]
            b_kp.append(kp)
            b_vp.append(vp)
            b_ip.append(kip)
            b_ks.append(ks)
            b_vs.append(vs)
            b_is.append(kis)
        xp = xp + mp[2] * op
        xs = xs + ms[2] * os_
        hp = rmsnorm(xp, g_ffn[i]) * (1 + mp[4]) + mp[3]
        hs = rmsnorm(xs, g_ffn[i]) * (1 + ms[4]) + ms[3]
        xp = xp + mp[5] * moe(hp, w_router[i], b_router[i], w_gate[i], w_up[i], w_down[i],
                              ws_gate[i], ws_up[i], ws_down[i])
        xs = xs + ms[5] * moe(hs, w_router[i], b_router[i], w_gate[i], w_up[i], w_down[i],
                              ws_gate[i], ws_up[i], ws_down[i])
    return (xp, xs,
            jnp.stack(a_kp), jnp.stack(a_vp), jnp.stack(a_ks), jnp.stack(a_vs),
            jnp.stack(b_kp), jnp.stack(b_vp), jnp.stack(b_ip),
            jnp.stack(b_ks), jnp.stack(b_vs), jnp.stack(b_is))
```

```python
import functools
import math

import jax
import jax.numpy as jnp
import numpy as np
from jax import lax
from jax.experimental import pallas as pl
from jax.experimental.pallas import tpu as pltpu

F32 = jnp.float32
BF16 = jnp.bfloat16
I32 = jnp.int32

A_HEADS = 8
A_HEAD_DIM = 128
B_HEADS = 16
B_HEAD_DIM = 128
IDX_HEADS = 16
IDX_DIM = 128
IDX_TOPK_MAX = 256
N_EXPERTS = 64
TOP_K = 8
N_GROUPS = 8
TOPK_GROUPS = 4
ROUTED_SCALE = 2.5
ROPE_THETA = 10000.0
EPS = 1e-6

LANES = 128
SUBLANES = 8
V7X_VMEM_LIMIT_BYTES = 56 * 1024 * 1024

NEG_INF = float("-inf")


def _cparams(n_grid, **kw):
    return pltpu.CompilerParams(dimension_semantics=("arbitrary",) * n_grid,
                                vmem_limit_bytes=V7X_VMEM_LIMIT_BYTES, **kw)


def _nt_dot(a, b):
    return lax.dot_general(a, b, (((1,), (1,)), ((), ())), preferred_element_type=F32)


def _dot(a, b):
    return jnp.dot(a, b, preferred_element_type=F32)


def _silu(x):
    return x * jax.nn.sigmoid(x)


def _mod_kernel(c_ref, w_ref, b_ref, o_ref):
    a = _silu(c_ref[...]).astype(BF16)
    o_ref[...] = _dot(a, w_ref[...].astype(BF16)) + b_ref[...]


def _mod_call(c_all, w_ada, b_ada):
    n_layers, d, n = w_ada.shape
    r = c_all.shape[0]
    tn = 1024
    return pl.pallas_call(
        _mod_kernel,
        grid=(n_layers, n // tn),
        in_specs=[pl.BlockSpec((r, d), lambda l, j: (0, 0)),
                  pl.BlockSpec((None, d, tn), lambda l, j: (l, 0, j)),
                  pl.BlockSpec((None, 1, tn), lambda l, j: (l, 0, j))],
        out_specs=pl.BlockSpec((None, r, tn), lambda l, j: (l, 0, j)),
        out_shape=jax.ShapeDtypeStruct((n_layers, r, n), F32),
        compiler_params=_cparams(2),
        name="adaln_modulation",
    )(c_all, w_ada, b_ada.reshape(n_layers, 1, n))


def _normmod(x, g, scale, shift):
    y = x * lax.rsqrt(jnp.mean(x * x, axis=-1, keepdims=True) + EPS) * g
    return y * (1.0 + scale) + shift


def _normmod_kernel(x_ref, g_ref, sc_ref, sh_ref, o_ref):
    o_ref[...] = _normmod(x_ref[...], g_ref[...], sc_ref[...], sh_ref[...]).astype(o_ref.dtype)


def _mod_spec(mod, tm, rows_per_mod):
    _, r, d = mod.shape
    tiles = rows_per_mod // tm
    return pl.BlockSpec((None, r, d), lambda i: (i // tiles, 0, 0))


def _normmod_call(x, g, scale, shift, rows_per_mod, tm):
    m, d = x.shape
    return pl.pallas_call(
        _normmod_kernel,
        grid=(m // tm,),
        in_specs=[pl.BlockSpec((tm, d), lambda i: (i, 0)),
                  pl.BlockSpec((1, d), lambda i: (0, 0)),
                  _mod_spec(scale, tm, rows_per_mod),
                  _mod_spec(shift, tm, rows_per_mod)],
        out_specs=pl.BlockSpec((tm, d), lambda i: (i, 0)),
        out_shape=jax.ShapeDtypeStruct((m, d), BF16),
        compiler_params=_cparams(1),
        name="norm_modulate",
    )(x, g.reshape(1, d), scale, shift)


def _rope_tables(pos, d):
    inv = jnp.exp(-math.log(ROPE_THETA) * jnp.arange(0, d, 2, dtype=jnp.float32) / d)
    ang = pos.astype(jnp.float32)[:, None] * inv[None, :]
    cos, sin = jnp.cos(ang), jnp.sin(ang)
    return jnp.concatenate([cos, cos], axis=-1), jnp.concatenate([-sin, sin], axis=-1)


def _proj_kernel(*refs, kinds, has_gain, has_rope, out_f32, out_bf16, out_scale):
    refs = list(refs)
    a_ref, w_ref = refs[0], refs[1]
    pos = 2
    gain_ref = cos_ref = sin_ref = None
    if has_gain:
        gain_ref = refs[pos]
        pos += 1
    if has_rope:
        cos_ref, sin_ref = refs[pos], refs[pos + 1]
        pos += 2
    outs = refs[pos:-1]
    wb_ref = refs[-1]

    @pl.when(pl.program_id(1) == 0)
    def _():
        wb_ref[...] = w_ref[...].astype(BF16)

    acc = _dot(a_ref[...], wb_ref[...])
    cw = acc.shape[1] // len(kinds)
    for c, kind in enumerate(kinds):
        sl = slice(c * cw, (c + 1) * cw)
        x = acc[:, sl]
        if kind == "norm_rope":
            x = x * lax.rsqrt(jnp.mean(x * x, axis=-1, keepdims=True) + EPS) * gain_ref[...]
        if kind in ("norm_rope", "rope"):
            x = x * cos_ref[...] + pltpu.roll(x, LANES // 2, 1) * sin_ref[...]
        if out_scale != 1.0:
            x = x * out_scale
        k = 0
        if out_f32:
            outs[k][:, sl] = x
            k += 1
        if out_bf16:
            outs[k][:, sl] = x.astype(BF16)


def _proj_call(a, w, col0, n, kinds_one, *, gain=None, rope=None, rope_rows=None, out_f32=True,
               out_bf16=False, out_scale=1.0, tm=512, tn=1024):
    m, k = a.shape
    tm = min(tm, m)
    if rope is not None and rope_rows > 1:
        tm = min(tm, rope_rows)
    tn = min(tn, n)
    cw = min(LANES, tn)
    assert m % tm == 0 and n % tn == 0 and col0 % tn == 0 and tn % cw == 0
    jb = col0 // tn
    kinds = tuple(kinds_one[(c * cw) // (n // len(kinds_one))] for c in range(n // cw))
    per_tile = [kinds[j * (tn // cw):(j + 1) * (tn // cw)] for j in range(n // tn)]
    assert all(p == per_tile[0] for p in per_tile)
    in_specs = [pl.BlockSpec((tm, k), lambda j, i: (i, 0)),
                pl.BlockSpec((k, tn), lambda j, i: (0, j + jb))]
    args = [a, w]
    if gain is not None:
        in_specs.append(pl.BlockSpec((1, LANES), lambda j, i: (0, 0)))
        args.append(gain.reshape(1, LANES))
    if rope is not None:
        if rope_rows == 1:
            spec = pl.BlockSpec((1, LANES), lambda j, i: (0, 0))
        else:
            tiles = rope_rows // tm
            spec = pl.BlockSpec((tm, LANES), lambda j, i: (i % tiles, 0))
        in_specs += [spec, spec]
        args += [rope[0], rope[1]]
    out_shape, out_specs = [], []
    for flag, dt in ((out_f32, F32), (out_bf16, BF16)):
        if flag:
            out_shape.append(jax.ShapeDtypeStruct((m, n), dt))
            out_specs.append(pl.BlockSpec((tm, tn), lambda j, i: (i, j)))
    res = pl.pallas_call(
        functools.partial(_proj_kernel, kinds=per_tile[0], has_gain=gain is not None,
                          has_rope=rope is not None, out_f32=out_f32, out_bf16=out_bf16,
                          out_scale=out_scale),
        grid=(n // tn, m // tm),
        in_specs=in_specs,
        out_specs=out_specs,
        out_shape=out_shape,
        scratch_shapes=[pltpu.VMEM((k, tn), BF16)],
        compiler_params=_cparams(2),
        name="projection",
    )(*args)
    return res if len(res) > 1 else res[0]


def _resid_kernel(a_ref, w_ref, x_ref, gate_ref, o_ref, wb_ref):
    @pl.when(pl.program_id(1) == 0)
    def _():
        wb_ref[...] = w_ref[...].astype(BF16)

    o_ref[...] = x_ref[...] + gate_ref[...] * _dot(a_ref[...], wb_ref[...])


def _resid_call(a, w, x, gate, rows_per_mod, tm=512, tn=1024):
    m, k = a.shape
    n = w.shape[1]
    tm = min(tm, m, rows_per_mod)
    tiles = rows_per_mod // tm
    r = gate.shape[1]
    return pl.pallas_call(
        _resid_kernel,
        grid=(n // tn, m // tm),
        in_specs=[pl.BlockSpec((tm, k), lambda j, i: (i, 0)),
                  pl.BlockSpec((k, tn), lambda j, i: (0, j)),
                  pl.BlockSpec((tm, tn), lambda j, i: (i, j)),
                  pl.BlockSpec((None, r, tn), lambda j, i: (i // tiles, 0, j))],
        out_specs=pl.BlockSpec((tm, tn), lambda j, i: (i, j)),
        out_shape=jax.ShapeDtypeStruct((m, n), F32),
        scratch_shapes=[pltpu.VMEM((k, tn), BF16)],
        compiler_params=_cparams(2),
        name="out_projection_residual",
    )(a, w, x, gate)


def _diff_lambda(lq1_ref, lk1_ref, lq2_ref, lk2_ref, lam_init):
    s1 = jnp.sum(lq1_ref[...] * lk1_ref[...], axis=-1, keepdims=True)
    s2 = jnp.sum(lq2_ref[...] * lk2_ref[...], axis=-1, keepdims=True)
    return jnp.exp(s1) - jnp.exp(s2) + lam_init


def _subln(o, subln, lam_init):
    o = o * lax.rsqrt(jnp.mean(o * o, axis=-1, keepdims=True) + EPS) * subln
    return o * (1.0 - lam_init)


def _diff_flash_kernel(q_ref, k_ref, v_ref, lq1_ref, lk1_ref, lq2_ref, lk2_ref, subln_ref, o_ref,
                       m_ref, l_ref, acc_ref, *, tq, lam_init):
    qi = pl.program_id(2)
    ki = pl.program_id(3)
    d = A_HEAD_DIM
    scale = d ** -0.5

    @pl.when(ki == 0)
    def _():
        m_ref[...] = jnp.full(m_ref.shape, NEG_INF, F32)
        l_ref[...] = jnp.zeros(l_ref.shape, F32)
        acc_ref[...] = jnp.zeros(acc_ref.shape, F32)

    @pl.when(ki <= qi)
    def _():
        q = q_ref[...]
        k = k_ref[...]
        v = v_ref[...]
        row = qi * tq + lax.broadcasted_iota(I32, (tq, tq), 0)
        col = ki * tq + lax.broadcasted_iota(I32, (tq, tq), 1)
        keep = col <= row
        for c in range(2):
            s = _nt_dot(q[:, c * d:(c + 1) * d], k[:, c * d:(c + 1) * d]) * scale
            s = jnp.where(keep, s, NEG_INF)
            m_prev = m_ref[c]
            m_new = jnp.maximum(m_prev, jnp.max(s, axis=-1, keepdims=True))
            alpha = jnp.exp(m_prev - m_new)
            p = jnp.exp(s - m_new)
            l_ref[c] = alpha * l_ref[c] + jnp.sum(p, axis=-1, keepdims=True)
            acc_ref[c] = alpha * acc_ref[c] + _dot(p.astype(BF16), v)
            m_ref[c] = m_new

    @pl.when(ki == pl.num_programs(3) - 1)
    def _():
        lam = _diff_lambda(lq1_ref, lk1_ref, lq2_ref, lk2_ref, lam_init)
        o = acc_ref[0] / l_ref[0] - lam * (acc_ref[1] / l_ref[1])
        o_ref[...] = _subln(o, subln_ref[...], lam_init).astype(o_ref.dtype)


def _diff_flash_call(q, k, v, lams, subln, batch, seq, lam_init, tq=512):
    tq = min(tq, seq)
    nq = seq // tq
    hd = 2 * A_HEAD_DIM
    vec = pl.BlockSpec((1, A_HEAD_DIM), lambda b, h, i, j: (0, 0))
    return pl.pallas_call(
        functools.partial(_diff_flash_kernel, tq=tq, lam_init=lam_init),
        grid=(batch, A_HEADS, nq, nq),
        in_specs=[pl.BlockSpec((tq, hd), lambda b, h, i, j: (b * nq + i, h)),
                  pl.BlockSpec((tq, hd), lambda b, h, i, j: (b * nq + jnp.minimum(i, j), h)),
                  pl.BlockSpec((tq, hd), lambda b, h, i, j: (b * nq + jnp.minimum(i, j), h)),
                  vec, vec, vec, vec,
                  pl.BlockSpec((1, hd), lambda b, h, i, j: (0, 0))],
        out_specs=pl.BlockSpec((tq, hd), lambda b, h, i, j: (b * nq + i, h)),
        out_shape=jax.ShapeDtypeStruct(q.shape, BF16),
        scratch_shapes=[pltpu.VMEM((2, tq, 1), F32), pltpu.VMEM((2, tq, 1), F32),
                        pltpu.VMEM((2, tq, hd), F32)],
        compiler_params=_cparams(4),
        name="diff_attention_prompt",
    )(q, k, v, *[x.reshape(1, A_HEAD_DIM) for x in lams], subln.reshape(1, hd))


def _diff_decode_kernel(pt_ref, q_ref, kn_ref, vn_ref, lq1_ref, lk1_ref, lq2_ref, lk2_ref, subln_ref,
                        *rest, pages, page, lam_init):
    del pt_ref
    k_refs = rest[:pages]
    v_refs = rest[pages:2 * pages]
    o_ref = rest[2 * pages]
    m_ref, l_ref, acc_ref = rest[2 * pages + 1:]
    step = pl.program_id(1)
    d = A_HEAD_DIM
    h = A_HEADS
    scale = d ** -0.5
    q = q_ref[...]
    lane = lax.broadcasted_iota(I32, (h, 2 * d), 1)
    qbd = jnp.concatenate([jnp.where(lane < d, q, 0.0), jnp.where(lane >= d, q, 0.0)], axis=0)

    @pl.when(step == 0)
    def _():
        prod = q * kn_ref[...]
        s0 = jnp.sum(jnp.where(lane < d, prod, 0.0), axis=-1, keepdims=True)
        s1 = jnp.sum(jnp.where(lane >= d, prod, 0.0), axis=-1, keepdims=True)
        m_ref[...] = jnp.concatenate([s0, s1], axis=0) * scale
        l_ref[...] = jnp.ones(l_ref.shape, F32)
        acc_ref[...] = jnp.concatenate([vn_ref[...], vn_ref[...]], axis=0)

    n = page * h
    own = (lax.broadcasted_iota(I32, (2 * h, n), 1) % h) == (lax.broadcasted_iota(I32, (2 * h, n), 0) % h)
    qb = qbd.astype(BF16)
    s_list = []
    for r in range(pages):
        kf = k_refs[r][...].reshape(n, 2 * d).astype(BF16)
        s = _nt_dot(qb, kf) * scale
        s_list.append(jnp.where(own, s, NEG_INF))
    m_prev = m_ref[...]
    m_new = m_prev
    for s in s_list:
        m_new = jnp.maximum(m_new, jnp.max(s, axis=-1, keepdims=True))
    alpha = jnp.exp(m_prev - m_new)
    l_new = alpha * l_ref[...]
    acc = alpha * acc_ref[...]
    for r in range(pages):
        p = jnp.exp(s_list[r] - m_new)
        l_new = l_new + jnp.sum(p, axis=-1, keepdims=True)
        vf = v_refs[r][...].reshape(n, 2 * d).astype(BF16)
        acc = acc + _dot(p.astype(BF16), vf)
    m_ref[...] = m_new
    l_ref[...] = l_new
    acc_ref[...] = acc

    @pl.when(step == pl.num_programs(1) - 1)
    def _():
        lam = _diff_lambda(lq1_ref, lk1_ref, lq2_ref, lk2_ref, lam_init)
        o = acc[:h] / l_new[:h] - lam * (acc[h:] / l_new[h:])
        o_ref[...] = _subln(o, subln_ref[...], lam_init).astype(o_ref.dtype)


def _diff_decode_call(q, k_new, v_new, cache_k, cache_v, page_table, lams, subln, lam_init, pages=4):
    bd, n_pages = page_table.shape
    _, page, h, hd = cache_k.shape
    pages = min(pages, n_pages)
    assert n_pages % pages == 0
    row = pl.BlockSpec((None, h, hd), lambda b, s, pt: (b, 0, 0))
    vec = pl.BlockSpec((1, A_HEAD_DIM), lambda b, s, pt: (0, 0))

    def page_spec(r):
        return pl.BlockSpec((None, page, h, hd), lambda b, s, pt: (pt[b, s * pages + r], 0, 0, 0))

    grid_spec = pltpu.PrefetchScalarGridSpec(
        num_scalar_prefetch=1,
        grid=(bd, n_pages // pages),
        in_specs=[row, row, row, vec, vec, vec, vec, pl.BlockSpec((1, hd), lambda b, s, pt: (0, 0))]
        + [page_spec(r) for r in range(pages)] + [page_spec(r) for r in range(pages)],
        out_specs=row,
        scratch_shapes=[pltpu.VMEM((2 * h, 1), F32), pltpu.VMEM((2 * h, 1), F32),
                        pltpu.VMEM((2 * h, hd), F32)],
    )
    return pl.pallas_call(
        functools.partial(_diff_decode_kernel, pages=pages, page=page, lam_init=lam_init),
        grid_spec=grid_spec,
        out_shape=jax.ShapeDtypeStruct((bd, h, hd), BF16),
        compiler_params=_cparams(2),
        name="diff_attention_sample",
    )(page_table, q, k_new, v_new, *[x.reshape(1, A_HEAD_DIM) for x in lams], subln.reshape(1, hd),
      *([cache_k] * pages), *([cache_v] * pages))


def _index_prompt_kernel(qi_ref, wi_ref, ki_ref, o_ref, *, tq):
    i = pl.program_id(1)
    j = pl.program_id(2)

    @pl.when(j > i)
    def _():
        o_ref[...] = jnp.full(o_ref.shape, NEG_INF, F32)

    @pl.when(j <= i)
    def _():
        ki = ki_ref[...]
        wi = wi_ref[...]
        acc = jnp.zeros((tq, tq), F32)
        for h in range(IDX_HEADS):
            s = _nt_dot(qi_ref[:, h * IDX_DIM:(h + 1) * IDX_DIM], ki)
            acc = acc + jnp.maximum(s, 0.0) * wi[:, h:h + 1]
        acc = acc * (IDX_DIM ** -0.5)
        row = i * tq + lax.broadcasted_iota(I32, (tq, tq), 0)
        col = j * tq + lax.broadcasted_iota(I32, (tq, tq), 1)
        o_ref[...] = jnp.where(col <= row, acc, NEG_INF)


def _index_prompt_call(qi, wi, ki, batch, seq, tq=512):
    tq = min(tq, seq)
    nq = seq // tq
    return pl.pallas_call(
        functools.partial(_index_prompt_kernel, tq=tq),
        grid=(batch, nq, nq),
        in_specs=[pl.BlockSpec((tq, IDX_HEADS * IDX_DIM), lambda b, i, j: (b * nq + i, 0)),
                  pl.BlockSpec((tq, IDX_HEADS), lambda b, i, j: (b * nq + i, 0)),
                  pl.BlockSpec((tq, IDX_DIM), lambda b, i, j: (b * nq + jnp.minimum(i, j), 0))],
        out_specs=pl.BlockSpec((tq, tq), lambda b, i, j: (b * nq + i, j)),
        out_shape=jax.ShapeDtypeStruct((batch * seq, seq), F32),
        compiler_params=_cparams(3),
        name="dsa_index_scores_prompt",
    )(qi, wi, ki)


def _float_key(x):
    b = pltpu.bitcast(x, I32)
    return jnp.where(b < 0, b ^ jnp.int32(0x7FFFFFFF), b)


def _kth_kernel(x_ref, o_ref, *, kth):
    key = _float_key(x_ref[...])
    rows = key.shape[0]

    def count_ge(t):
        return jnp.sum(jnp.where(key >= t, 1.0, 0.0), axis=-1, keepdims=True)

    int_min = jnp.int32(-2 ** 31)
    t0 = jnp.where(count_ge(jnp.zeros((rows, 1), I32)) >= kth, jnp.int32(0), int_min)

    def body(b, t):
        cand = t + (jnp.int32(1) << (jnp.int32(30) - b))
        return jnp.where(count_ge(cand) >= kth, cand, t)

    t = lax.fori_loop(0, 31, body, t0)
    bits = jnp.where(t < 0, t ^ jnp.int32(0x7FFFFFFF), t)
    o_ref[...] = pltpu.bitcast(bits, F32)


def _kth_call(x, kth, tr=256):
    r, c = x.shape
    tr = min(tr, r)
    return pl.pallas_call(
        functools.partial(_kth_kernel, kth=float(kth)),
        grid=(r // tr,),
        in_specs=[pl.BlockSpec((tr, c), lambda i: (i, 0))],
        out_specs=pl.BlockSpec((tr, 1), lambda i: (i, 0)),
        out_shape=jax.ShapeDtypeStruct((r, 1), F32),
        compiler_params=_cparams(1),
        name="kth_largest",
    )(x)


def _dsa_prompt_kernel(q_ref, k_ref, v_ref, sc_ref, thr_ref, o_ref):
    d = B_HEAD_DIM
    scale = d ** -0.5
    sc = sc_ref[...]
    sel = jnp.logical_and(sc >= thr_ref[...], sc > NEG_INF)
    k = k_ref[...]
    v = v_ref[...]
    for h in range(B_HEADS):
        s = _nt_dot(q_ref[:, h * d:(h + 1) * d], k) * scale
        s = jnp.where(sel, s, NEG_INF)
        m = jnp.max(s, axis=-1, keepdims=True)
        p = jnp.exp(s - m)
        l = jnp.sum(p, axis=-1, keepdims=True)
        o_ref[:, h * d:(h + 1) * d] = (_dot(p.astype(BF16), v) / l).astype(o_ref.dtype)


def _dsa_prompt_call(q, kv, scores, thr, batch, seq, tq=256):
    tq = min(tq, seq)
    nq = seq // tq
    d = B_HEAD_DIM
    return pl.pallas_call(
        _dsa_prompt_kernel,
        grid=(batch, nq),
        in_specs=[pl.BlockSpec((tq, B_HEADS * d), lambda b, i: (b * nq + i, 0)),
                  pl.BlockSpec((seq, d), lambda b, i: (b, 0)),
                  pl.BlockSpec((seq, d), lambda b, i: (b, 1)),
                  pl.BlockSpec((tq, seq), lambda b, i: (b * nq + i, 0)),
                  pl.BlockSpec((tq, 1), lambda b, i: (b * nq + i, 0))],
        out_specs=pl.BlockSpec((tq, B_HEADS * d), lambda b, i: (b * nq + i, 0)),
        out_shape=jax.ShapeDtypeStruct(q.shape, BF16),
        compiler_params=_cparams(2),
        name="dsa_attention_prompt",
    )(q, kv, kv, scores, thr)


def _index_sample_kernel(pt_ref, qi_ref, wi_ref, kin_ref, *rest, pages, page):
    del pt_ref
    pg_refs = rest[:pages]
    o_ref, on_ref = rest[pages], rest[pages + 1]
    qi = qi_ref[...]
    wi = wi_ref[...]
    qb = qi.astype(BF16)
    for r in range(pages):
        s = _nt_dot(qb, pg_refs[r][...].astype(BF16))
        row = jnp.sum(jnp.maximum(s, 0.0) * wi, axis=0, keepdims=True) * (IDX_DIM ** -0.5)
        o_ref[:, r * page:(r + 1) * page] = row

    @pl.when(pl.program_id(1) == 0)
    def _():
        s = jnp.sum(qi * kin_ref[...], axis=-1, keepdims=True)
        sn = jnp.sum(jnp.maximum(s, 0.0) * wi, axis=0, keepdims=True) * (IDX_DIM ** -0.5)
        lane = lax.broadcasted_iota(I32, (1, LANES), 1)
        on_ref[...] = jnp.where(lane == 0, sn, NEG_INF)


def _index_sample_call(qi, wi, ki_new, cache_idx, page_table, pages=16):
    bd, n_pages = page_table.shape
    _, page, d = cache_idx.shape
    pages = min(pages, n_pages)
    assert n_pages % pages == 0

    def page_spec(r):
        return pl.BlockSpec((None, page, d), lambda b, s, pt: (pt[b, s * pages + r], 0, 0))

    grid_spec = pltpu.PrefetchScalarGridSpec(
        num_scalar_prefetch=1,
        grid=(bd, n_pages // pages),
        in_specs=[pl.BlockSpec((None, IDX_HEADS, d), lambda b, s, pt: (b, 0, 0)),
                  pl.BlockSpec((None, IDX_HEADS, 1), lambda b, s, pt: (b, 0, 0)),
                  pl.BlockSpec((None, 1, d), lambda b, s, pt: (b, 0, 0))]
        + [page_spec(r) for r in range(pages)],
        out_specs=[pl.BlockSpec((None, 1, pages * page), lambda b, s, pt: (b, 0, s)),
                   pl.BlockSpec((None, 1, LANES), lambda b, s, pt: (b, 0, 0))],
    )
    return pl.pallas_call(
        functools.partial(_index_sample_kernel, pages=pages, page=page),
        grid_spec=grid_spec,
        out_shape=[jax.ShapeDtypeStruct((bd, 1, n_pages * page), F32),
                   jax.ShapeDtypeStruct((bd, 1, LANES), F32)],
        compiler_params=_cparams(2),
        name="dsa_index_scores_sample",
    )(page_table, qi, wi, ki_new, *([cache_idx] * pages))


def _dsa_sample_kernel(pt_ref, q_ref, kn_ref, vn_ref, sc_ref, scn_ref, thr_ref, *rest, pages, page):
    del pt_ref
    k_refs = rest[:pages]
    v_refs = rest[pages:2 * pages]
    o_ref = rest[2 * pages]
    m_ref, l_ref, acc_ref = rest[2 * pages + 1:]
    d = B_HEAD_DIM
    scale = d ** -0.5
    q = q_ref[...]
    thr = thr_ref[...]

    @pl.when(pl.program_id(1) == 0)
    def _():
        s = jnp.sum(q * kn_ref[...], axis=-1, keepdims=True) * scale
        scn = scn_ref[...][:, 0:1]
        sel = jnp.logical_and(scn >= thr, scn > NEG_INF)
        m_ref[...] = jnp.where(sel, s, -1e30)
        l_ref[...] = jnp.where(sel, 1.0, 0.0) * jnp.ones(l_ref.shape, F32)
        acc_ref[...] = jnp.where(sel, 1.0, 0.0) * (jnp.ones((B_HEADS, 1), F32) * vn_ref[...])

    qb = q.astype(BF16)
    m_run = m_ref[...]
    l_run = l_ref[...]
    acc = acc_ref[...]
    for r in range(pages):
        sc = sc_ref[:, r * page:(r + 1) * page]
        sel = jnp.logical_and(sc >= thr, sc > NEG_INF)
        s = _nt_dot(qb, k_refs[r][...].astype(BF16)) * scale
        s = jnp.where(sel, s, -1e30)
        m_new = jnp.maximum(m_run, jnp.max(s, axis=-1, keepdims=True))
        alpha = jnp.exp(m_run - m_new)
        p = jnp.where(sel, jnp.exp(s - m_new), 0.0)
        l_run = alpha * l_run + jnp.sum(p, axis=-1, keepdims=True)
        acc = alpha * acc + _dot(p.astype(BF16), v_refs[r][...].astype(BF16))
        m_run = m_new
    m_ref[...] = m_run
    l_ref[...] = l_run
    acc_ref[...] = acc

    @pl.when(pl.program_id(1) == pl.num_programs(1) - 1)
    def _():
        o_ref[...] = (acc / l_run).astype(o_ref.dtype)


def _dsa_sample_call(q, k_new, v_new, scores, score_new, thr, cache_k, cache_v, page_table, pages=16):
    bd, n_pages = page_table.shape
    _, page, d = cache_k.shape
    pages = min(pages, n_pages)
    assert n_pages % pages == 0

    def page_spec(r):
        return pl.BlockSpec((None, page, d), lambda b, s, pt: (pt[b, s * pages + r], 0, 0))

    one = pl.BlockSpec((None, 1, d), lambda b, s, pt: (b, 0, 0))
    grid_spec = pltpu.PrefetchScalarGridSpec(
        num_scalar_prefetch=1,
        grid=(bd, n_pages // pages),
        in_specs=[pl.BlockSpec((None, B_HEADS, d), lambda b, s, pt: (b, 0, 0)), one, one,
                  pl.BlockSpec((None, 1, pages * page), lambda b, s, pt: (b, 0, s)),
                  pl.BlockSpec((None, 1, LANES), lambda b, s, pt: (b, 0, 0)),
                  pl.BlockSpec((None, 1, 1), lambda b, s, pt: (b, 0, 0))]
        + [page_spec(r) for r in range(pages)] + [page_spec(r) for r in range(pages)],
        out_specs=pl.BlockSpec((None, B_HEADS, d), lambda b, s, pt: (b, 0, 0)),
        scratch_shapes=[pltpu.VMEM((B_HEADS, 1), F32), pltpu.VMEM((B_HEADS, 1), F32),
                        pltpu.VMEM((B_HEADS, d), F32)],
    )
    return pl.pallas_call(
        functools.partial(_dsa_sample_kernel, pages=pages, page=page),
        grid_spec=grid_spec,
        out_shape=jax.ShapeDtypeStruct((bd, B_HEADS, d), BF16),
        compiler_params=_cparams(2),
        name="dsa_attention_sample",
    )(page_table, q, k_new, v_new, scores, score_new, thr, *([cache_k] * pages), *([cache_v] * pages))


def _first_argmax(v, lane, width):
    m = jnp.max(v, axis=-1, keepdims=True)
    idx = jnp.min(jnp.where(v == m, lane, width), axis=-1, keepdims=True)
    return m, idx


def _router_kernel(x_ref, g_ref, sc_ref, sh_ref, wr_ref, br_ref, h_ref, eidx_ref, wgt_ref):
    h = _normmod(x_ref[...], g_ref[...], sc_ref[...], sh_ref[...])
    h_ref[...] = h
    w = wr_ref[...]
    h_hi = h.astype(BF16)
    h_lo = (h - h_hi.astype(F32)).astype(BF16)
    w_hi = w.astype(BF16)
    w_lo = (w - w_hi.astype(F32)).astype(BF16)
    logits = _dot(h_hi, w_hi) + (_dot(h_hi, w_lo) + _dot(h_lo, w_hi))
    scores = jax.nn.sigmoid(logits)
    biased = scores + br_ref[...]
    tm = scores.shape[0]
    lane = lax.broadcasted_iota(I32, (tm, N_EXPERTS), 1)
    per = N_EXPERTS // N_GROUPS
    grp = lane // per
    gs = []
    for g in range(N_GROUPS):
        vg = jnp.where(grp == g, biased, NEG_INF)
        m1, i1 = _first_argmax(vg, lane, N_EXPERTS)
        m2 = jnp.max(jnp.where(lane == i1, NEG_INF, vg), axis=-1, keepdims=True)
        gs.append(m1 + m2)
    emask = jnp.zeros((tm, N_EXPERTS), jnp.bool_)
    for g in range(N_GROUPS):
        beat = jnp.zeros((tm, 1), F32)
        for o in range(N_GROUPS):
            if o == g:
                continue
            wins = (gs[o] >= gs[g]) if o < g else (gs[o] > gs[g])
            beat = beat + jnp.where(wins, 1.0, 0.0)
        emask = jnp.logical_or(emask, jnp.logical_and(grp == g, beat < TOPK_GROUPS))
    masked = jnp.where(emask, biased, NEG_INF)
    lane_k = lax.broadcasted_iota(I32, (tm, TOP_K), 1)
    eidx = jnp.zeros((tm, TOP_K), I32)
    wsel = jnp.zeros((tm, TOP_K), F32)
    for it in range(TOP_K):
        _, i = _first_argmax(masked, lane, N_EXPERTS)
        hit = lane == i
        wv = jnp.sum(jnp.where(hit, scores, 0.0), axis=-1, keepdims=True)
        eidx = jnp.where(lane_k == it, i, eidx)
        wsel = jnp.where(lane_k == it, wv, wsel)
        masked = jnp.where(hit, NEG_INF, masked)
    eidx_ref[...] = eidx
    wgt_ref[...] = wsel / jnp.sum(wsel, axis=-1, keepdims=True) * ROUTED_SCALE


def _router_call(x, g, scale, shift, w_router, b_router, rows_per_mod, tm=256):
    m, d = x.shape
    tm = min(tm, m, rows_per_mod)
    return pl.pallas_call(
        _router_kernel,
        grid=(m // tm,),
        in_specs=[pl.BlockSpec((tm, d), lambda i: (i, 0)),
                  pl.BlockSpec((1, d), lambda i: (0, 0)),
                  _mod_spec(scale, tm, rows_per_mod),
                  _mod_spec(shift, tm, rows_per_mod),
                  pl.BlockSpec((d, N_EXPERTS), lambda i: (0, 0)),
                  pl.BlockSpec((1, N_EXPERTS), lambda i: (0, 0))],
        out_specs=[pl.BlockSpec((tm, d), lambda i: (i, 0)),
                   pl.BlockSpec((tm, TOP_K), lambda i: (i, 0)),
                   pl.BlockSpec((tm, TOP_K), lambda i: (i, 0))],
        out_shape=[jax.ShapeDtypeStruct((m, d), F32),
                   jax.ShapeDtypeStruct((m, TOP_K), I32),
                   jax.ShapeDtypeStruct((m, TOP_K), F32)],
        compiler_params=_cparams(1),
        name="ffn_norm_router",
    )(x, g.reshape(1, d), scale, shift, w_router, b_router.reshape(1, N_EXPERTS))


EXPERT_TILE = 256


def _gather_rows(idx_ref, src_hbm, dst, sem, n_rows):
    def body(r, carry):
        pltpu.make_async_copy(src_hbm.at[pl.ds(idx_ref[0, r], 1), :], dst.at[pl.ds(r, 1), :], sem).start()
        return carry

    lax.fori_loop(0, n_rows, body, 0)


def _wait_rows(src_hbm, dst, sem, n_rows):
    pltpu.make_async_copy(src_hbm.at[pl.ds(0, n_rows), :], dst, sem).wait()


def _expert_kernel(te_ref, tf_ref, nu_ref, cur_ref, nxt_ref, h_hbm, wg_ref, wu_ref, wd_ref, o_ref,
                   xbuf, sems, wgb, wub, wdb):
    del te_ref
    i = pl.program_id(0)
    n_used = nu_ref[0]
    tm = EXPERT_TILE
    slot = i % 2

    @pl.when(i == 0)
    def _():
        _gather_rows(cur_ref, h_hbm, xbuf.at[0], sems.at[0], tm)

    @pl.when(i + 1 < n_used)
    def _():
        _gather_rows(nxt_ref, h_hbm, xbuf.at[1 - slot], sems.at[1 - slot], tm)

    @pl.when(tf_ref[i] == 1)
    def _():
        wgb[...] = wg_ref[...].astype(BF16)
        wub[...] = wu_ref[...].astype(BF16)
        wdb[...] = wd_ref[...].astype(BF16)

    @pl.when(i < n_used)
    def _():
        _wait_rows(h_hbm, xbuf.at[slot], sems.at[slot], tm)
        x = xbuf[slot].astype(BF16)
        a = (_silu(_dot(x, wgb[...])) * _dot(x, wub[...])).astype(BF16)
        o_ref[...] = _dot(a, wdb[...])

    @pl.when(i >= n_used)
    def _():
        o_ref[...] = jnp.zeros(o_ref.shape, F32)


def _expert_call(h_all, src_tok, tile_expert, tile_first, n_used, w_gate, w_up, w_down):
    n_tiles = tile_expert.shape[0]
    tm = EXPERT_TILE
    t_all, d = h_all.shape
    _, _, de = w_gate.shape
    src = src_tok.reshape(n_tiles, 1, tm)

    def last(i, nu):
        return jnp.minimum(i, nu[0] - 1)

    grid_spec = pltpu.PrefetchScalarGridSpec(
        num_scalar_prefetch=3,
        grid=(n_tiles,),
        in_specs=[pl.BlockSpec((None, 1, tm), lambda i, te, tf, nu: (last(i, nu), 0, 0),
                               memory_space=pltpu.SMEM),
                  pl.BlockSpec((None, 1, tm), lambda i, te, tf, nu: (last(i + 1, nu), 0, 0),
                               memory_space=pltpu.SMEM),
                  pl.BlockSpec(memory_space=pl.ANY),
                  pl.BlockSpec((None, d, de), lambda i, te, tf, nu: (te[last(i, nu)], 0, 0)),
                  pl.BlockSpec((None, d, de), lambda i, te, tf, nu: (te[last(i, nu)], 0, 0)),
                  pl.BlockSpec((None, de, d), lambda i, te, tf, nu: (te[last(i, nu)], 0, 0))],
        out_specs=pl.BlockSpec((tm, d), lambda i, te, tf, nu: (i, 0)),
        scratch_shapes=[pltpu.VMEM((2, tm, d), F32), pltpu.SemaphoreType.DMA((2,)),
                        pltpu.VMEM((d, de), BF16), pltpu.VMEM((d, de), BF16), pltpu.VMEM((de, d), BF16)],
    )
    return pl.pallas_call(
        _expert_kernel,
        grid_spec=grid_spec,
        out_shape=jax.ShapeDtypeStruct((n_tiles * tm, d), F32),
        compiler_params=_cparams(1, disable_bounds_checks=True),
        name="routed_experts",
    )(tile_expert, tile_first, n_used, src, src, h_all, w_gate, w_up, w_down)


def _combine_kernel(pos_ref, pos_nxt_ref, x_ref, h_ref, wgt_ref, gate_ref, sg_ref, su_ref, sd_ref, y_hbm,
                    o_ref, ybuf, sems, *, tc):
    i = pl.program_id(0)
    n = pl.num_programs(0)
    slot = i % 2
    rows = tc * TOP_K

    @pl.when(i == 0)
    def _():
        _gather_rows(pos_ref, y_hbm, ybuf.at[0], sems.at[0], rows)

    @pl.when(i + 1 < n)
    def _():
        _gather_rows(pos_nxt_ref, y_hbm, ybuf.at[1 - slot], sems.at[1 - slot], rows)

    hb = h_ref[...].astype(BF16)
    a = (_silu(_dot(hb, sg_ref[...])) * _dot(hb, su_ref[...])).astype(BF16)
    moe = _dot(a, sd_ref[...])
    _wait_rows(y_hbm, ybuf.at[slot], sems.at[slot], rows)
    wgt = wgt_ref[...]
    for k in range(TOP_K):
        moe = moe + wgt[:, k:k + 1] * ybuf[slot, k * tc:(k + 1) * tc, :]
    o_ref[...] = x_ref[...] + gate_ref[...] * moe


def _combine_call(x, h, pos, wgt, gate, ws_gate, ws_up, ws_down, y_sorted, rows_per_mod, tc=128):
    m, d = x.shape
    tc = min(tc, m, rows_per_mod)
    n_tiles = m // tc
    ds = ws_gate.shape[1]
    pos_t = pos.reshape(n_tiles, tc, TOP_K).transpose(0, 2, 1).reshape(n_tiles, 1, tc * TOP_K)
    grid_spec = pltpu.PrefetchScalarGridSpec(
        num_scalar_prefetch=0,
        grid=(n_tiles,),
        in_specs=[pl.BlockSpec((None, 1, tc * TOP_K), lambda i: (i, 0, 0), memory_space=pltpu.SMEM),
                  pl.BlockSpec((None, 1, tc * TOP_K), lambda i: (jnp.minimum(i + 1, n_tiles - 1), 0, 0),
                               memory_space=pltpu.SMEM),
                  pl.BlockSpec((tc, d), lambda i: (i, 0)),
                  pl.BlockSpec((tc, d), lambda i: (i, 0)),
                  pl.BlockSpec((tc, TOP_K), lambda i: (i, 0)),
                  _mod_spec(gate, tc, rows_per_mod),
                  pl.BlockSpec((d, ds), lambda i: (0, 0)),
                  pl.BlockSpec((d, ds), lambda i: (0, 0)),
                  pl.BlockSpec((ds, d), lambda i: (0, 0)),
                  pl.BlockSpec(memory_space=pl.ANY)],
        out_specs=pl.BlockSpec((tc, d), lambda i: (i, 0)),
        scratch_shapes=[pltpu.VMEM((2, tc * TOP_K, d), F32), pltpu.SemaphoreType.DMA((2,))],
    )
    return pl.pallas_call(
        functools.partial(_combine_kernel, tc=tc),
        grid_spec=grid_spec,
        out_shape=jax.ShapeDtypeStruct((m, d), F32),
        compiler_params=_cparams(1, disable_bounds_checks=True),
        name="moe_combine_residual",
    )(pos_t, pos_t, x, h, wgt, gate, ws_gate, ws_up, ws_down, y_sorted)


def _dispatch_plan(eidx):
    t, kk = eidx.shape
    tm = EXPERT_TILE
    n_pairs = t * kk
    n_tiles = (n_pairs + N_EXPERTS * (tm - 1)) // tm
    e_flat = eidx.reshape(n_pairs)
    order = jnp.argsort(e_flat, stable=True).astype(I32)
    counts = jnp.sum(jax.nn.one_hot(e_flat, N_EXPERTS, dtype=I32), axis=0)
    gsz = (counts + tm - 1) // tm * tm
    gend = jnp.cumsum(gsz)
    gstart = gend - gsz
    cstart = jnp.cumsum(counts) - counts
    rank_sorted = jnp.arange(n_pairs, dtype=I32) - cstart[e_flat[order]]
    slot_sorted = gstart[e_flat[order]] + rank_sorted
    inv = jnp.argsort(order).astype(I32)
    pos = slot_sorted[inv].reshape(t, kk)
    tile_start = jnp.arange(n_tiles, dtype=I32) * tm
    tile_expert = jnp.minimum(jnp.searchsorted(gend, tile_start, side="right"), N_EXPERTS - 1).astype(I32)
    n_used = (gend[-1] // tm).astype(I32).reshape(1)
    slots = jnp.arange(n_tiles * tm, dtype=I32)
    e_slot = jnp.repeat(tile_expert, tm)
    j = cstart[e_slot] + jnp.clip(slots - gstart[e_slot], 0, jnp.maximum(counts[e_slot] - 1, 0))
    src_tok = order[jnp.clip(j, 0, n_pairs - 1)] // kk
    prev = jnp.concatenate([jnp.full((1,), -1, I32), tile_expert[:-1]])
    tile_first = (tile_expert != prev).astype(I32)
    return src_tok.astype(I32), pos.astype(I32), tile_expert, tile_first, n_used


def _moe_layer(xp, xs, mp, ms, g_ffn, w_router, b_router, w_gate, w_up, w_down, ws_gate, ws_up, ws_down, seq):
    bd = xs.shape[0]
    hp, ep, wp = _router_call(xp, g_ffn, mp[4], mp[3], w_router, b_router, seq)
    hs, es, ws = _router_call(xs, g_ffn, ms[4], ms[3], w_router, b_router, bd)
    h_all = jnp.concatenate([hp, hs], axis=0)
    src_tok, pos, tile_expert, tile_first, n_used = _dispatch_plan(jnp.concatenate([ep, es], axis=0))
    y = _expert_call(h_all, src_tok, tile_expert, tile_first, n_used, w_gate, w_up, w_down)
    tp = xp.shape[0]
    shared = (ws_gate.astype(BF16), ws_up.astype(BF16), ws_down.astype(BF16))
    xp = _combine_call(xp, hp, pos[:tp], wp, mp[5], *shared, y, seq)
    xs = _combine_call(xs, hs, pos[tp:], ws, ms[5], *shared, y, bd)
    return xp, xs


def kernel(x_prompt, x_sample, cache_a_k, cache_a_v, cache_b_k, cache_b_v, cache_b_idx, page_table,
           c_prompt, c_sample, w_ada, b_ada, g_mix, g_ffn, a_w_in, a_q_norm, a_k_norm, a_lambda_q1,
           a_lambda_k1, a_lambda_q2, a_lambda_k2, a_subln, a_w_out, b_w_in, b_q_norm, b_k_norm, b_w_out,
           w_router, b_router, w_gate, w_up, w_down, ws_gate, ws_up, ws_down):
    batch, seq, d = x_prompt.shape
    bd, ts, _ = x_sample.shape
    assert ts == 1
    depth = w_ada.shape[0]
    page = cache_a_k.shape[2]
    past = page_table.shape[1] * page
    tp = batch * seq

    cos_p, sin_p = _rope_tables(jnp.arange(seq, dtype=jnp.int32), LANES)
    cos_s, sin_s = _rope_tables(past + jnp.arange(ts, dtype=jnp.int32), LANES)

    n_c = batch + bd
    pad = (-n_c) % SUBLANES
    c_all = jnp.concatenate([c_prompt, c_sample, jnp.zeros((pad, d), F32)], axis=0)
    mod = _mod_call(c_all, w_ada, b_ada).reshape(depth, n_c + pad, 6, d)

    xp = x_prompt.reshape(tp, d)
    xs = x_sample.reshape(bd, d)
    outs_a = [[], [], [], []]
    outs_b = [[], [], [], [], [], []]
    for i in range(depth):
        mp = [mod[i, :batch, t].reshape(batch, 1, d) for t in range(6)]
        ms = [mod[i, batch:n_c, t].reshape(1, bd, d) for t in range(6)]
        hp = _normmod_call(xp, g_mix[i], mp[1], mp[0], seq, min(512, seq))
        hs = _normmod_call(xs, g_mix[i], ms[1], ms[0], bd, bd)
        j = i // 2
        if i % 2 == 0:
            lam_init = 0.8 - 0.6 * math.exp(-0.3 * i)
            lams = (a_lambda_q1[j], a_lambda_k1[j], a_lambda_q2[j], a_lambda_k2[j])
            w_in = a_w_in[j]
            mix = A_HEADS * 2 * A_HEAD_DIM
            qp = _proj_call(hp, w_in, 0, mix, ("norm_rope",), gain=a_q_norm[j], rope=(cos_p, sin_p),
                            rope_rows=seq, out_f32=False, out_bf16=True)
            kp, kp_b = _proj_call(hp, w_in, mix, mix, ("norm_rope",), gain=a_k_norm[j], rope=(cos_p, sin_p),
                                  rope_rows=seq, out_bf16=True)
            vp, vp_b = _proj_call(hp, w_in, 2 * mix, mix, ("plain",), out_bf16=True)
            qs = _proj_call(hs, w_in, 0, mix, ("norm_rope",), gain=a_q_norm[j], rope=(cos_s, sin_s), rope_rows=1)
            ks = _proj_call(hs, w_in, mix, mix, ("norm_rope",), gain=a_k_norm[j], rope=(cos_s, sin_s), rope_rows=1)
            vs = _proj_call(hs, w_in, 2 * mix, mix, ("plain",))
            op = _diff_flash_call(qp, kp_b, vp_b, lams, a_subln[j], batch, seq, lam_init)
            hd = 2 * A_HEAD_DIM
            os_ = _diff_decode_call(qs.reshape(bd, A_HEADS, hd), ks.reshape(bd, A_HEADS, hd),
                                    vs.reshape(bd, A_HEADS, hd), cache_a_k[j], cache_a_v[j], page_table,
                                    lams, a_subln[j], lam_init).reshape(bd, mix)
            w_out = a_w_out[j]
            outs_a[0].append(kp.reshape(batch, seq, A_HEADS, hd))
            outs_a[1].append(vp.reshape(batch, seq, A_HEADS, hd))
            outs_a[2].append(ks.reshape(bd, ts, A_HEADS, hd))
            outs_a[3].append(vs.reshape(bd, ts, A_HEADS, hd))
        else:
            w_in = b_w_in[j]
            mix = B_HEADS * B_HEAD_DIM
            dh = B_HEAD_DIM
            o_kv = mix
            o_qi = mix + 2 * dh
            o_ki = o_qi + IDX_HEADS * IDX_DIM
            o_wi = o_ki + IDX_DIM
            w_wi = w_in[:, o_wi:o_wi + IDX_HEADS]
            wi_scale = IDX_HEADS ** -0.5

            def project(h, rope, rope_rows, bf16):
                q = _proj_call(h, w_in, 0, mix, ("norm_rope",), gain=b_q_norm[j], rope=rope, rope_rows=rope_rows,
                               out_f32=not bf16, out_bf16=bf16)
                kv = _proj_call(h, w_in, o_kv, 2 * dh, ("norm_rope", "plain"), gain=b_k_norm[j], rope=rope,
                                rope_rows=rope_rows, out_bf16=bf16, tn=2 * dh)
                qi = _proj_call(h, w_in, o_qi, IDX_HEADS * IDX_DIM, ("rope",), rope=rope, rope_rows=rope_rows,
                                out_f32=not bf16, out_bf16=bf16, tm=1024, tn=256)
                ki = _proj_call(h, w_in, o_ki, IDX_DIM, ("rope",), rope=rope, rope_rows=rope_rows,
                                out_bf16=bf16, tn=IDX_DIM)
                wi = _proj_call(h, w_wi, 0, IDX_HEADS, ("plain",), out_scale=wi_scale, tn=IDX_HEADS)
                return q, kv, qi, ki, wi

            qp, (kvp, kvp_b), qip, (kip, kip_b), wip = project(hp, (cos_p, sin_p), seq, True)
            qs, kvs, qis, kis, wis = project(hs, (cos_s, sin_s), 1, False)
            n_sel = min(IDX_TOPK_MAX, seq // 4)
            sc_p = _index_prompt_call(qip, wip, kip_b, batch, seq)
            thr_p = _kth_call(sc_p, n_sel)
            op = _dsa_prompt_call(qp, kvp_b, sc_p, thr_p, batch, seq)
            sc_s, sc_new = _index_sample_call(qis.reshape(bd, IDX_HEADS, IDX_DIM), wis.reshape(bd, IDX_HEADS, 1),
                                              kis.reshape(bd, 1, IDX_DIM), cache_b_idx[j], page_table)
            n_sel_s = min(IDX_TOPK_MAX, (past + ts) // 4)
            thr_s = _kth_call(jnp.concatenate([sc_s.reshape(bd, past), sc_new.reshape(bd, LANES)], axis=1),
                              n_sel_s, tr=bd)
            os_ = _dsa_sample_call(qs.reshape(bd, B_HEADS, dh), kvs[:, :dh].reshape(bd, 1, dh),
                                   kvs[:, dh:].reshape(bd, 1, dh), sc_s, sc_new, thr_s.reshape(bd, 1, 1),
                                   cache_b_k[j], cache_b_v[j], page_table).reshape(bd, mix)
            w_out = b_w_out[j]
            outs_b[0].append(kvp[:, :dh].reshape(batch, seq, dh))
            outs_b[1].append(kvp[:, dh:].reshape(batch, seq, dh))
            outs_b[2].append(kip.reshape(batch, seq, IDX_DIM))
            outs_b[3].append(kvs[:, :dh].reshape(bd, ts, dh))
            outs_b[4].append(kvs[:, dh:].reshape(bd, ts, dh))
            outs_b[5].append(kis.reshape(bd, ts, IDX_DIM))
        xp = _resid_call(op, w_out, xp, mp[2], seq)
        xs = _resid_call(os_.astype(BF16), w_out, xs, ms[2], bd)
        xp, xs = _moe_layer(xp, xs, mp, ms, g_ffn[i], w_router[i], b_router[i], w_gate[i], w_up[i], w_down[i],
                            ws_gate[i], ws_up[i], ws_down[i], seq)
    return (xp.reshape(batch, seq, d), xs.reshape(bd, ts, d),
            *[jnp.stack(o) for o in outs_a], *[jnp.stack(o) for o in outs_b])
```

```python
import functools
import math

import jax
import jax.numpy as jnp
import numpy as np
from jax import lax
from jax.experimental import pallas as pl
from jax.experimental.pallas import tpu as pltpu

F32 = jnp.float32
BF16 = jnp.bfloat16
I32 = jnp.int32

A_HEADS = 8
A_HEAD_DIM = 128
B_HEADS = 16
B_HEAD_DIM = 128
IDX_HEADS = 16
IDX_DIM = 128
IDX_TOPK_MAX = 256
N_EXPERTS = 64
TOP_K = 8
N_GROUPS = 8
TOPK_GROUPS = 4
ROUTED_SCALE = 2.5
ROPE_THETA = 10000.0
EPS = 1e-6

LANES = 128
SUBLANES = 8
V7X_VMEM_LIMIT_BYTES = 56 * 1024 * 1024

NEG_INF = float("-inf")


def _cparams(n_grid, **kw):
    return pltpu.CompilerParams(dimension_semantics=("arbitrary",) * n_grid,
                                vmem_limit_bytes=V7X_VMEM_LIMIT_BYTES, **kw)


def _nt_dot(a, b):
    return lax.dot_general(a, b, (((1,), (1,)), ((), ())), preferred_element_type=F32)


def _dot(a, b):
    return jnp.dot(a, b, preferred_element_type=F32)


def _silu(x):
    return x * jax.nn.sigmoid(x)


def _mod_kernel(c_ref, w_ref, b_ref, o_ref):
    a = _silu(c_ref[...]).astype(BF16)
    o_ref[...] = _dot(a, w_ref[...].astype(BF16)) + b_ref[...]


def _mod_call(c_all, w_ada, b_ada):
    n_layers, d, n = w_ada.shape
    r = c_all.shape[0]
    tn = 1024
    return pl.pallas_call(
        _mod_kernel,
        grid=(n_layers, n // tn),
        in_specs=[pl.BlockSpec((r, d), lambda l, j: (0, 0)),
                  pl.BlockSpec((None, d, tn), lambda l, j: (l, 0, j)),
                  pl.BlockSpec((None, 1, tn), lambda l, j: (l, 0, j))],
        out_specs=pl.BlockSpec((None, r, tn), lambda l, j: (l, 0, j)),
        out_shape=jax.ShapeDtypeStruct((n_layers, r, n), F32),
        compiler_params=_cparams(2),
        name="adaln_modulation",
    )(c_all, w_ada, b_ada.reshape(n_layers, 1, n))


def _normmod(x, g, scale, shift):
    y = x * lax.rsqrt(jnp.mean(x * x, axis=-1, keepdims=True) + EPS) * g
    return y * (1.0 + scale) + shift


def _normmod_kernel(x_ref, g_ref, sc_ref, sh_ref, o_ref):
    o_ref[...] = _normmod(x_ref[...], g_ref[...], sc_ref[...], sh_ref[...]).astype(o_ref.dtype)


def _mod_spec(mod, tm, rows_per_mod):
    _, r, d = mod.shape
    tiles = rows_per_mod // tm
    return pl.BlockSpec((None, r, d), lambda i: (i // tiles, 0, 0))


def _normmod_call(x, g, scale, shift, rows_per_mod, tm):
    m, d = x.shape
    return pl.pallas_call(
        _normmod_kernel,
        grid=(m // tm,),
        in_specs=[pl.BlockSpec((tm, d), lambda i: (i, 0)),
                  pl.BlockSpec((1, d), lambda i: (0, 0)),
                  _mod_spec(scale, tm, rows_per_mod),
                  _mod_spec(shift, tm, rows_per_mod)],
        out_specs=pl.BlockSpec((tm, d), lambda i: (i, 0)),
        out_shape=jax.ShapeDtypeStruct((m, d), BF16),
        compiler_params=_cparams(1),
        name="norm_modulate",
    )(x, g.reshape(1, d), scale, shift)


def _rope_tables(pos, d):
    inv = jnp.exp(-math.log(ROPE_THETA) * jnp.arange(0, d, 2, dtype=jnp.float32) / d)
    ang = pos.astype(jnp.float32)[:, None] * inv[None, :]
    cos, sin = jnp.cos(ang), jnp.sin(ang)
    return jnp.concatenate([cos, cos], axis=-1), jnp.concatenate([-sin, sin], axis=-1)


def _proj_kernel(*refs, kinds, has_gain, has_rope, out_f32, out_bf16, out_scale):
    refs = list(refs)
    a_ref, w_ref = refs[0], refs[1]
    pos = 2
    gain_ref = cos_ref = sin_ref = None
    if has_gain:
        gain_ref = refs[pos]
        pos += 1
    if has_rope:
        cos_ref, sin_ref = refs[pos], refs[pos + 1]
        pos += 2
    outs = refs[pos:-1]
    wb_ref = refs[-1]

    @pl.when(pl.program_id(1) == 0)
    def _():
        wb_ref[...] = w_ref[...].astype(BF16)

    acc = _dot(a_ref[...], wb_ref[...])
    cw = acc.shape[1] // len(kinds)
    for c, kind in enumerate(kinds):
        sl = slice(c * cw, (c + 1) * cw)
        x = acc[:, sl]
        if kind == "norm_rope":
            x = x * lax.rsqrt(jnp.mean(x * x, axis=-1, keepdims=True) + EPS) * gain_ref[...]
        if kind in ("norm_rope", "rope"):
            x = x * cos_ref[...] + pltpu.roll(x, LANES // 2, 1) * sin_ref[...]
        if out_scale != 1.0:
            x = x * out_scale
        k = 0
        if out_f32:
            outs[k][:, sl] = x
            k += 1
        if out_bf16:
            outs[k][:, sl] = x.astype(BF16)


def _proj_call(a, w, col0, n, kinds_one, *, gain=None, rope=None, rope_rows=None, out_f32=True,
               out_bf16=False, out_scale=1.0, tm=512, tn=1024):
    m, k = a.shape
    tm = min(tm, m)
    if rope is not None and rope_rows > 1:
        tm = min(tm, rope_rows)
    tn = min(tn, n)
    cw = min(LANES, tn)
    assert m % tm == 0 and n % tn == 0 and col0 % tn == 0 and tn % cw == 0
    jb = col0 // tn
    kinds = tuple(kinds_one[(c * cw) // (n // len(kinds_one))] for c in range(n // cw))
    per_tile = [kinds[j * (tn // cw):(j + 1) * (tn // cw)] for j in range(n // tn)]
    assert all(p == per_tile[0] for p in per_tile)
    in_specs = [pl.BlockSpec((tm, k), lambda j, i: (i, 0)),
                pl.BlockSpec((k, tn), lambda j, i: (0, j + jb))]
    args = [a, w]
    if gain is not None:
        in_specs.append(pl.BlockSpec((1, LANES), lambda j, i: (0, 0)))
        args.append(gain.reshape(1, LANES))
    if rope is not None:
        if rope_rows == 1:
            spec = pl.BlockSpec((1, LANES), lambda j, i: (0, 0))
        else:
            tiles = rope_rows // tm
            spec = pl.BlockSpec((tm, LANES), lambda j, i: (i % tiles, 0))
        in_specs += [spec, spec]
        args += [rope[0], rope[1]]
    out_shape, out_specs = [], []
    for flag, dt in ((out_f32, F32), (out_bf16, BF16)):
        if flag:
            out_shape.append(jax.ShapeDtypeStruct((m, n), dt))
            out_specs.append(pl.BlockSpec((tm, tn), lambda j, i: (i, j)))
    res = pl.pallas_call(
        functools.partial(_proj_kernel, kinds=per_tile[0], has_gain=gain is not None,
                          has_rope=rope is not None, out_f32=out_f32, out_bf16=out_bf16,
                          out_scale=out_scale),
        grid=(n // tn, m // tm),
        in_specs=in_specs,
        out_specs=out_specs,
        out_shape=out_shape,
        scratch_shapes=[pltpu.VMEM((k, tn), BF16)],
        compiler_params=_cparams(2),
        name="projection",
    )(*args)
    return res if len(res) > 1 else res[0]


def _resid_kernel(a_ref, w_ref, x_ref, gate_ref, o_ref, wb_ref):
    @pl.when(pl.program_id(1) == 0)
    def _():
        wb_ref[...] = w_ref[...].astype(BF16)

    o_ref[...] = x_ref[...] + gate_ref[...] * _dot(a_ref[...], wb_ref[...])


def _resid_call(a, w, x, gate, rows_per_mod, tm=512, tn=1024):
    m, k = a.shape
    n = w.shape[1]
    tm = min(tm, m, rows_per_mod)
    tiles = rows_per_mod // tm
    r = gate.shape[1]
    return pl.pallas_call(
        _resid_kernel,
        grid=(n // tn, m // tm),
        in_specs=[pl.BlockSpec((tm, k), lambda j, i: (i, 0)),
                  pl.BlockSpec((k, tn), lambda j, i: (0, j)),
                  pl.BlockSpec((tm, tn), lambda j, i: (i, j)),
                  pl.BlockSpec((None, r, tn), lambda j, i: (i // tiles, 0, j))],
        out_specs=pl.BlockSpec((tm, tn), lambda j, i: (i, j)),
        out_shape=jax.ShapeDtypeStruct((m, n), F32),
        scratch_shapes=[pltpu.VMEM((k, tn), BF16)],
        compiler_params=_cparams(2),
        name="out_projection_residual",
    )(a, w, x, gate)


def _diff_lambda(lq1_ref, lk1_ref, lq2_ref, lk2_ref, lam_init):
    s1 = jnp.sum(lq1_ref[...] * lk1_ref[...], axis=-1, keepdims=True)
    s2 = jnp.sum(lq2_ref[...] * lk2_ref[...], axis=-1, keepdims=True)
    return jnp.exp(s1) - jnp.exp(s2) + lam_init


def _subln(o, subln, lam_init):
    o = o * lax.rsqrt(jnp.mean(o * o, axis=-1, keepdims=True) + EPS) * subln
    return o * (1.0 - lam_init)


def _diff_flash_kernel(q_ref, k_ref, v_ref, lq1_ref, lk1_ref, lq2_ref, lk2_ref, subln_ref, o_ref,
                       m_ref, l_ref, acc_ref, *, tq, lam_init):
    qi = pl.program_id(2)
    ki = pl.program_id(3)
    d = A_HEAD_DIM
    scale = d ** -0.5

    @pl.when(ki == 0)
    def _():
        m_ref[...] = jnp.full(m_ref.shape, NEG_INF, F32)
        l_ref[...] = jnp.zeros(l_ref.shape, F32)
        acc_ref[...] = jnp.zeros(acc_ref.shape, F32)

    def update(diagonal):
        q = q_ref[...]
        k = k_ref[...]
        v = v_ref[...]
        for c in range(2):
            s = _nt_dot(q[:, c * d:(c + 1) * d], k[:, c * d:(c + 1) * d]) * scale
            if diagonal:
                keep = lax.broadcasted_iota(I32, (tq, tq), 1) <= lax.broadcasted_iota(I32, (tq, tq), 0)
                s = jnp.where(keep, s, NEG_INF)
            m_prev = m_ref[c]
            m_new = jnp.maximum(m_prev, jnp.max(s, axis=-1, keepdims=True))
            alpha = jnp.exp(m_prev - m_new)
            p = jnp.exp(s - m_new)
            l_ref[c] = alpha * l_ref[c] + jnp.sum(p, axis=-1, keepdims=True)
            acc_ref[c] = alpha * acc_ref[c] + _dot(p.astype(BF16), v)
            m_ref[c] = m_new

    pl.when(ki < qi)(functools.partial(update, False))
    pl.when(ki == qi)(functools.partial(update, True))

    @pl.when(ki == pl.num_programs(3) - 1)
    def _():
        lam = _diff_lambda(lq1_ref, lk1_ref, lq2_ref, lk2_ref, lam_init)
        o = acc_ref[0] / l_ref[0] - lam * (acc_ref[1] / l_ref[1])
        o_ref[...] = _subln(o, subln_ref[...], lam_init).astype(o_ref.dtype)


def _diff_flash_call(q, k, v, lams, subln, batch, seq, lam_init, tq=512):
    tq = min(tq, seq)
    nq = seq // tq
    hd = 2 * A_HEAD_DIM
    vec = pl.BlockSpec((1, A_HEAD_DIM), lambda b, h, i, j: (0, 0))
    return pl.pallas_call(
        functools.partial(_diff_flash_kernel, tq=tq, lam_init=lam_init),
        grid=(batch, A_HEADS, nq, nq),
        in_specs=[pl.BlockSpec((tq, hd), lambda b, h, i, j: (b * nq + i, h)),
                  pl.BlockSpec((tq, hd), lambda b, h, i, j: (b * nq + jnp.minimum(i, j), h)),
                  pl.BlockSpec((tq, hd), lambda b, h, i, j: (b * nq + jnp.minimum(i, j), h)),
                  vec, vec, vec, vec,
                  pl.BlockSpec((1, hd), lambda b, h, i, j: (0, 0))],
        out_specs=pl.BlockSpec((tq, hd), lambda b, h, i, j: (b * nq + i, h)),
        out_shape=jax.ShapeDtypeStruct(q.shape, BF16),
        scratch_shapes=[pltpu.VMEM((2, tq, 1), F32), pltpu.VMEM((2, tq, 1), F32),
                        pltpu.VMEM((2, tq, hd), F32)],
        compiler_params=_cparams(4),
        name="diff_attention_prompt",
    )(q, k, v, *[x.reshape(1, A_HEAD_DIM) for x in lams], subln.reshape(1, hd))


def _diff_decode_kernel(pt_ref, q_ref, kn_ref, vn_ref, lq1_ref, lk1_ref, lq2_ref, lk2_ref, subln_ref,
                        *rest, pages, page, lam_init):
    del pt_ref
    k_refs = rest[:pages]
    v_refs = rest[pages:2 * pages]
    o_ref = rest[2 * pages]
    m_ref, l_ref, acc_ref = rest[2 * pages + 1:]
    step = pl.program_id(1)
    d = A_HEAD_DIM
    h = A_HEADS
    scale = d ** -0.5
    q = q_ref[...]
    lane = lax.broadcasted_iota(I32, (h, 2 * d), 1)
    qbd = jnp.concatenate([jnp.where(lane < d, q, 0.0), jnp.where(lane >= d, q, 0.0)], axis=0)

    @pl.when(step == 0)
    def _():
        prod = q * kn_ref[...]
        s0 = jnp.sum(jnp.where(lane < d, prod, 0.0), axis=-1, keepdims=True)
        s1 = jnp.sum(jnp.where(lane >= d, prod, 0.0), axis=-1, keepdims=True)
        m_ref[...] = jnp.concatenate([s0, s1], axis=0) * scale
        l_ref[...] = jnp.ones(l_ref.shape, F32)
        acc_ref[...] = jnp.concatenate([vn_ref[...], vn_ref[...]], axis=0)

    n = page * h
    own = (lax.broadcasted_iota(I32, (2 * h, n), 1) % h) == (lax.broadcasted_iota(I32, (2 * h, n), 0) % h)
    qb = qbd.astype(BF16)
    s_list = []
    for r in range(pages):
        kf = k_refs[r][...].reshape(n, 2 * d).astype(BF16)
        s = _nt_dot(qb, kf) * scale
        s_list.append(jnp.where(own, s, NEG_INF))
    m_prev = m_ref[...]
    m_new = m_prev
    for s in s_list:
        m_new = jnp.maximum(m_new, jnp.max(s, axis=-1, keepdims=True))
    alpha = jnp.exp(m_prev - m_new)
    l_new = alpha * l_ref[...]
    acc = alpha * acc_ref[...]
    for r in range(pages):
        p = jnp.exp(s_list[r] - m_new)
        l_new = l_new + jnp.sum(p, axis=-1, keepdims=True)
        vf = v_refs[r][...].reshape(n, 2 * d).astype(BF16)
        acc = acc + _dot(p.astype(BF16), vf)
    m_ref[...] = m_new
    l_ref[...] = l_new
    acc_ref[...] = acc

    @pl.when(step == pl.num_programs(1) - 1)
    def _():
        lam = _diff_lambda(lq1_ref, lk1_ref, lq2_ref, lk2_ref, lam_init)
        o = acc[:h] / l_new[:h] - lam * (acc[h:] / l_new[h:])
        o_ref[...] = _subln(o, subln_ref[...], lam_init).astype(o_ref.dtype)


def _diff_decode_call(q, k_new, v_new, cache_k, cache_v, page_table, lams, subln, lam_init, pages=8):
    bd, n_pages = page_table.shape
    _, page, h, hd = cache_k.shape
    pages = min(pages, n_pages)
    assert n_pages % pages == 0
    row = pl.BlockSpec((None, h, hd), lambda b, s, pt: (b, 0, 0))
    vec = pl.BlockSpec((1, A_HEAD_DIM), lambda b, s, pt: (0, 0))

    def page_spec(r):
        return pl.BlockSpec((None, page, h, hd), lambda b, s, pt: (pt[b, s * pages + r], 0, 0, 0))

    grid_spec = pltpu.PrefetchScalarGridSpec(
        num_scalar_prefetch=1,
        grid=(bd, n_pages // pages),
        in_specs=[row, row, row, vec, vec, vec, vec, pl.BlockSpec((1, hd), lambda b, s, pt: (0, 0))]
        + [page_spec(r) for r in range(pages)] + [page_spec(r) for r in range(pages)],
        out_specs=row,
        scratch_shapes=[pltpu.VMEM((2 * h, 1), F32), pltpu.VMEM((2 * h, 1), F32),
                        pltpu.VMEM((2 * h, hd), F32)],
    )
    return pl.pallas_call(
        functools.partial(_diff_decode_kernel, pages=pages, page=page, lam_init=lam_init),
        grid_spec=grid_spec,
        out_shape=jax.ShapeDtypeStruct((bd, h, hd), BF16),
        compiler_params=_cparams(2),
        name="diff_attention_sample",
    )(page_table, q, k_new, v_new, *[x.reshape(1, A_HEAD_DIM) for x in lams], subln.reshape(1, hd),
      *([cache_k] * pages), *([cache_v] * pages))


def _index_prompt_kernel(qi_ref, wi_ref, ki_ref, o_ref, *, tq):
    i = pl.program_id(1)
    j = pl.program_id(2)

    @pl.when(j > i)
    def _():
        o_ref[...] = jnp.full(o_ref.shape, NEG_INF, F32)

    @pl.when(j <= i)
    def _():
        ki = ki_ref[...]
        wi = wi_ref[...]
        acc = jnp.zeros((tq, tq), F32)
        for h in range(IDX_HEADS):
            s = _nt_dot(qi_ref[:, h * IDX_DIM:(h + 1) * IDX_DIM], ki)
            acc = acc + jnp.maximum(s, 0.0) * wi[:, h:h + 1]
        acc = acc * (IDX_DIM ** -0.5)
        row = i * tq + lax.broadcasted_iota(I32, (tq, tq), 0)
        col = j * tq + lax.broadcasted_iota(I32, (tq, tq), 1)
        o_ref[...] = jnp.where(col <= row, acc, NEG_INF)


def _index_prompt_call(qi, wi, ki, batch, seq, tq=512):
    tq = min(tq, seq)
    nq = seq // tq
    return pl.pallas_call(
        functools.partial(_index_prompt_kernel, tq=tq),
        grid=(batch, nq, nq),
        in_specs=[pl.BlockSpec((tq, IDX_HEADS * IDX_DIM), lambda b, i, j: (b * nq + i, 0)),
                  pl.BlockSpec((tq, IDX_HEADS), lambda b, i, j: (b * nq + i, 0)),
                  pl.BlockSpec((tq, IDX_DIM), lambda b, i, j: (b * nq + jnp.minimum(i, j), 0))],
        out_specs=pl.BlockSpec((tq, tq), lambda b, i, j: (b * nq + i, j)),
        out_shape=jax.ShapeDtypeStruct((batch * seq, seq), F32),
        compiler_params=_cparams(3),
        name="dsa_index_scores_prompt",
    )(qi, wi, ki)


def _float_key(x):
    b = pltpu.bitcast(x, I32)
    return jnp.where(b < 0, b ^ jnp.int32(0x7FFFFFFF), b)


def _causal_widths(tile, tq, seq, chunk, fn):
    chunk = min(chunk, seq)
    need = ((tile + 1) * tq + chunk - 1) // chunk
    for v in range(1, seq // chunk + 1):
        pl.when(need == v)(functools.partial(fn, v * chunk))


def _kth_kernel(x_ref, o_ref, *, kth, causal_seq):
    rows, cols = x_ref.shape
    int_min = jnp.int32(-2 ** 31)

    def search(width):
        key = _float_key(x_ref[:, :width])
        key_ninf = jnp.int32(-2 ** 31 + 0x7FFFFF)
        extra = float(cols - width)

        def count_ge(t):
            c = jnp.sum(jnp.where(key >= t, 1.0, 0.0), axis=-1, keepdims=True)
            return c + jnp.where(t <= key_ninf, extra, 0.0) if extra else c

        t0 = jnp.where(count_ge(jnp.zeros((rows, 1), I32)) >= kth, jnp.int32(0), int_min)

        def body(b, t):
            cand = t + (jnp.int32(1) << (jnp.int32(30) - b))
            return jnp.where(count_ge(cand) >= kth, cand, t)

        t = lax.fori_loop(0, 31, body, t0)
        bits = jnp.where(t < 0, t ^ jnp.int32(0x7FFFFFFF), t)
        o_ref[...] = pltpu.bitcast(bits, F32)

    if causal_seq is None:
        search(cols)
    else:
        _causal_widths(pl.program_id(0) % (causal_seq // rows), rows, causal_seq, 512, search)


def _kth_call(x, kth, tr=256, causal_seq=None):
    r, c = x.shape
    tr = min(tr, r)
    return pl.pallas_call(
        functools.partial(_kth_kernel, kth=float(kth), causal_seq=causal_seq),
        grid=(r // tr,),
        in_specs=[pl.BlockSpec((tr, c), lambda i: (i, 0))],
        out_specs=pl.BlockSpec((tr, 1), lambda i: (i, 0)),
        out_shape=jax.ShapeDtypeStruct((r, 1), F32),
        compiler_params=_cparams(1),
        name="kth_largest",
    )(x)


def _dsa_prompt_kernel(q_ref, k_ref, v_ref, sc_ref, thr_ref, o_ref):
    d = B_HEAD_DIM
    scale = d ** -0.5
    tq, seq = sc_ref.shape

    def attend(width):
        sc = sc_ref[:, :width]
        sel = jnp.logical_and(sc >= thr_ref[...], sc > NEG_INF)
        k = k_ref[:width, :]
        v = v_ref[:width, :]
        for h in range(B_HEADS):
            s = _nt_dot(q_ref[:, h * d:(h + 1) * d], k) * scale
            s = jnp.where(sel, s, NEG_INF)
            m = jnp.max(s, axis=-1, keepdims=True)
            p = jnp.exp(s - m)
            l = jnp.sum(p, axis=-1, keepdims=True)
            o_ref[:, h * d:(h + 1) * d] = (_dot(p.astype(BF16), v) / l).astype(o_ref.dtype)

    _causal_widths(pl.program_id(1), tq, seq, 512, attend)


def _dsa_prompt_call(q, kv, scores, thr, batch, seq, tq=256):
    tq = min(tq, seq)
    nq = seq // tq
    d = B_HEAD_DIM
    return pl.pallas_call(
        _dsa_prompt_kernel,
        grid=(batch, nq),
        in_specs=[pl.BlockSpec((tq, B_HEADS * d), lambda b, i: (b * nq + i, 0)),
                  pl.BlockSpec((seq, d), lambda b, i: (b, 0)),
                  pl.BlockSpec((seq, d), lambda b, i: (b, 1)),
                  pl.BlockSpec((tq, seq), lambda b, i: (b * nq + i, 0)),
                  pl.BlockSpec((tq, 1), lambda b, i: (b * nq + i, 0))],
        out_specs=pl.BlockSpec((tq, B_HEADS * d), lambda b, i: (b * nq + i, 0)),
        out_shape=jax.ShapeDtypeStruct(q.shape, BF16),
        compiler_params=_cparams(2),
        name="dsa_attention_prompt",
    )(q, kv, kv, scores, thr)


def _index_sample_kernel(pt_ref, qi_ref, wi_ref, kin_ref, *rest, pages, page):
    del pt_ref
    pg_refs = rest[:pages]
    o_ref, on_ref = rest[pages], rest[pages + 1]
    qi = qi_ref[...]
    wi = wi_ref[...]
    del page
    keys = jnp.concatenate([pg_refs[r][...].astype(BF16) for r in range(pages)], axis=0)
    s = _nt_dot(qi.astype(BF16), keys)
    o_ref[...] = jnp.sum(jnp.maximum(s, 0.0) * wi, axis=0, keepdims=True) * (IDX_DIM ** -0.5)

    @pl.when(pl.program_id(1) == 0)
    def _():
        s = jnp.sum(qi * kin_ref[...], axis=-1, keepdims=True)
        sn = jnp.sum(jnp.maximum(s, 0.0) * wi, axis=0, keepdims=True) * (IDX_DIM ** -0.5)
        lane = lax.broadcasted_iota(I32, (1, LANES), 1)
        on_ref[...] = jnp.where(lane == 0, sn, NEG_INF)


def _index_sample_call(qi, wi, ki_new, cache_idx, page_table, pages=16):
    bd, n_pages = page_table.shape
    _, page, d = cache_idx.shape
    pages = min(pages, n_pages)
    assert n_pages % pages == 0

    def page_spec(r):
        return pl.BlockSpec((None, page, d), lambda b, s, pt: (pt[b, s * pages + r], 0, 0))

    grid_spec = pltpu.PrefetchScalarGridSpec(
        num_scalar_prefetch=1,
        grid=(bd, n_pages // pages),
        in_specs=[pl.BlockSpec((None, IDX_HEADS, d), lambda b, s, pt: (b, 0, 0)),
                  pl.BlockSpec((None, IDX_HEADS, 1), lambda b, s, pt: (b, 0, 0)),
                  pl.BlockSpec((None, 1, d), lambda b, s, pt: (b, 0, 0))]
        + [page_spec(r) for r in range(pages)],
        out_specs=[pl.BlockSpec((None, 1, pages * page), lambda b, s, pt: (b, 0, s)),
                   pl.BlockSpec((None, 1, LANES), lambda b, s, pt: (b, 0, 0))],
    )
    return pl.pallas_call(
        functools.partial(_index_sample_kernel, pages=pages, page=page),
        grid_spec=grid_spec,
        out_shape=[jax.ShapeDtypeStruct((bd, 1, n_pages * page), F32),
                   jax.ShapeDtypeStruct((bd, 1, LANES), F32)],
        compiler_params=_cparams(2),
        name="dsa_index_scores_sample",
    )(page_table, qi, wi, ki_new, *([cache_idx] * pages))


def _dsa_sample_kernel(pt_ref, q_ref, kn_ref, vn_ref, sc_ref, scn_ref, thr_ref, *rest, pages, page):
    del pt_ref
    k_refs = rest[:pages]
    v_refs = rest[pages:2 * pages]
    o_ref = rest[2 * pages]
    m_ref, l_ref, acc_ref = rest[2 * pages + 1:]
    d = B_HEAD_DIM
    scale = d ** -0.5
    q = q_ref[...]
    thr = thr_ref[...]

    @pl.when(pl.program_id(1) == 0)
    def _():
        s = jnp.sum(q * kn_ref[...], axis=-1, keepdims=True) * scale
        scn = scn_ref[...][:, 0:1]
        sel = jnp.logical_and(scn >= thr, scn > NEG_INF)
        m_ref[...] = jnp.where(sel, s, -1e30)
        l_ref[...] = jnp.where(sel, 1.0, 0.0) * jnp.ones(l_ref.shape, F32)
        acc_ref[...] = jnp.where(sel, 1.0, 0.0) * (jnp.ones((B_HEADS, 1), F32) * vn_ref[...])

    del page
    keys = jnp.concatenate([k_refs[r][...].astype(BF16) for r in range(pages)], axis=0)
    vals = jnp.concatenate([v_refs[r][...].astype(BF16) for r in range(pages)], axis=0)
    sc = sc_ref[...]
    sel = jnp.logical_and(sc >= thr, sc > NEG_INF)
    s = jnp.where(sel, _nt_dot(q.astype(BF16), keys) * scale, -1e30)
    m_run = m_ref[...]
    m_new = jnp.maximum(m_run, jnp.max(s, axis=-1, keepdims=True))
    alpha = jnp.exp(m_run - m_new)
    p = jnp.where(sel, jnp.exp(s - m_new), 0.0)
    l_run = alpha * l_ref[...] + jnp.sum(p, axis=-1, keepdims=True)
    acc = alpha * acc_ref[...] + _dot(p.astype(BF16), vals)
    m_ref[...] = m_new
    l_ref[...] = l_run
    acc_ref[...] = acc

    @pl.when(pl.program_id(1) == pl.num_programs(1) - 1)
    def _():
        o_ref[...] = (acc / l_run).astype(o_ref.dtype)


def _dsa_sample_call(q, k_new, v_new, scores, score_new, thr, cache_k, cache_v, page_table, pages=16):
    bd, n_pages = page_table.shape
    _, page, d = cache_k.shape
    pages = min(pages, n_pages)
    assert n_pages % pages == 0

    def page_spec(r):
        return pl.BlockSpec((None, page, d), lambda b, s, pt: (pt[b, s * pages + r], 0, 0))

    one = pl.BlockSpec((None, 1, d), lambda b, s, pt: (b, 0, 0))
    grid_spec = pltpu.PrefetchScalarGridSpec(
        num_scalar_prefetch=1,
        grid=(bd, n_pages // pages),
        in_specs=[pl.BlockSpec((None, B_HEADS, d), lambda b, s, pt: (b, 0, 0)), one, one,
                  pl.BlockSpec((None, 1, pages * page), lambda b, s, pt: (b, 0, s)),
                  pl.BlockSpec((None, 1, LANES), lambda b, s, pt: (b, 0, 0)),
                  pl.BlockSpec((None, 1, 1), lambda b, s, pt: (b, 0, 0))]
        + [page_spec(r) for r in range(pages)] + [page_spec(r) for r in range(pages)],
        out_specs=pl.BlockSpec((None, B_HEADS, d), lambda b, s, pt: (b, 0, 0)),
        scratch_shapes=[pltpu.VMEM((B_HEADS, 1), F32), pltpu.VMEM((B_HEADS, 1), F32),
                        pltpu.VMEM((B_HEADS, d), F32)],
    )
    return pl.pallas_call(
        functools.partial(_dsa_sample_kernel, pages=pages, page=page),
        grid_spec=grid_spec,
        out_shape=jax.ShapeDtypeStruct((bd, B_HEADS, d), BF16),
        compiler_params=_cparams(2),
        name="dsa_attention_sample",
    )(page_table, q, k_new, v_new, scores, score_new, thr, *([cache_k] * pages), *([cache_v] * pages))


def _first_argmax(v, lane, width):
    m = jnp.max(v, axis=-1, keepdims=True)
    idx = jnp.min(jnp.where(v == m, lane, width), axis=-1, keepdims=True)
    return m, idx


def _router_kernel(x_ref, g_ref, sc_ref, sh_ref, wr_ref, br_ref, cnt0_ref, h_ref, eidx_ref, wgt_ref,
                   cnt_ref, cnt_sc):
    @pl.when(pl.program_id(0) == 0)
    def _():
        cnt_sc[...] = cnt0_ref[...]

    h = _normmod(x_ref[...], g_ref[...], sc_ref[...], sh_ref[...])
    h_ref[...] = h
    w = wr_ref[...]
    h_hi = h.astype(BF16)
    h_lo = (h - h_hi.astype(F32)).astype(BF16)
    w_hi = w.astype(BF16)
    w_lo = (w - w_hi.astype(F32)).astype(BF16)
    logits = _dot(h_hi, w_hi) + (_dot(h_hi, w_lo) + _dot(h_lo, w_hi))
    scores = jax.nn.sigmoid(logits)
    biased = scores + br_ref[...]
    tm = scores.shape[0]
    lane = lax.broadcasted_iota(I32, (tm, N_EXPERTS), 1)
    per = N_EXPERTS // N_GROUPS
    grp = lane // per
    gs = []
    for g in range(N_GROUPS):
        vg = jnp.where(grp == g, biased, NEG_INF)
        m1, i1 = _first_argmax(vg, lane, N_EXPERTS)
        m2 = jnp.max(jnp.where(lane == i1, NEG_INF, vg), axis=-1, keepdims=True)
        gs.append(m1 + m2)
    emask = jnp.zeros((tm, N_EXPERTS), jnp.bool_)
    for g in range(N_GROUPS):
        beat = jnp.zeros((tm, 1), F32)
        for o in range(N_GROUPS):
            if o == g:
                continue
            wins = (gs[o] >= gs[g]) if o < g else (gs[o] > gs[g])
            beat = beat + jnp.where(wins, 1.0, 0.0)
        emask = jnp.logical_or(emask, jnp.logical_and(grp == g, beat < TOPK_GROUPS))
    masked = jnp.where(emask, biased, NEG_INF)
    lane_k = lax.broadcasted_iota(I32, (tm, TOP_K), 1)
    eidx = jnp.zeros((tm, TOP_K), I32)
    wsel = jnp.zeros((tm, TOP_K), F32)
    chosen = jnp.zeros((tm, N_EXPERTS), F32)
    for it in range(TOP_K):
        _, i = _first_argmax(masked, lane, N_EXPERTS)
        hit = lane == i
        chosen = chosen + jnp.where(hit, 1.0, 0.0)
        wv = jnp.sum(jnp.where(hit, scores, 0.0), axis=-1, keepdims=True)
        eidx = jnp.where(lane_k == it, i, eidx)
        wsel = jnp.where(lane_k == it, wv, wsel)
        masked = jnp.where(hit, NEG_INF, masked)
    eidx_ref[...] = eidx
    wgt_ref[...] = wsel / jnp.sum(wsel, axis=-1, keepdims=True) * ROUTED_SCALE
    cnt_sc[...] = cnt_sc[...] + jnp.sum(chosen, axis=0, keepdims=True)
    cnt_ref[...] = cnt_sc[...]


def _router_call(x, g, scale, shift, w_router, b_router, cnt0, rows_per_mod, tm=256):
    m, d = x.shape
    tm = min(tm, m, rows_per_mod)
    return pl.pallas_call(
        _router_kernel,
        grid=(m // tm,),
        in_specs=[pl.BlockSpec((tm, d), lambda i: (i, 0)),
                  pl.BlockSpec((1, d), lambda i: (0, 0)),
                  _mod_spec(scale, tm, rows_per_mod),
                  _mod_spec(shift, tm, rows_per_mod),
                  pl.BlockSpec((d, N_EXPERTS), lambda i: (0, 0)),
                  pl.BlockSpec((1, N_EXPERTS), lambda i: (0, 0)),
                  pl.BlockSpec((1, N_EXPERTS), lambda i: (0, 0))],
        out_specs=[pl.BlockSpec((tm, d), lambda i: (i, 0)),
                   pl.BlockSpec((tm, TOP_K), lambda i: (i, 0)),
                   pl.BlockSpec((tm, TOP_K), lambda i: (i, 0)),
                   pl.BlockSpec((1, N_EXPERTS), lambda i: (0, 0))],
        out_shape=[jax.ShapeDtypeStruct((m, d), F32),
                   jax.ShapeDtypeStruct((m, TOP_K), I32),
                   jax.ShapeDtypeStruct((m, TOP_K), F32),
                   jax.ShapeDtypeStruct((1, N_EXPERTS), F32)],
        scratch_shapes=[pltpu.VMEM((1, N_EXPERTS), F32)],
        compiler_params=_cparams(1),
        name="ffn_norm_router",
    )(x, g.reshape(1, d), scale, shift, w_router, b_router.reshape(1, N_EXPERTS), cnt0)


EXPERT_TILE = 256


def _row_copies(n_rows, make):
    def body8(r8, carry):
        for u in range(SUBLANES):
            make(r8 * SUBLANES + u).start()
        return carry

    def body1(r, carry):
        make(r).start()
        return carry

    full = n_rows // SUBLANES
    lax.fori_loop(0, full, body8, 0)
    if not isinstance(n_rows, int) or n_rows % SUBLANES:
        lax.fori_loop(full * SUBLANES, n_rows, body1, 0)


def _gather_rows(idx_ref, src_hbm, dst, sem, n_rows):
    _row_copies(n_rows, lambda r: pltpu.make_async_copy(
        src_hbm.at[pl.ds(idx_ref[0, r], 1), :], dst.at[pl.ds(r, 1), :], sem))


def _scatter_rows(idx_ref, src, dst_hbm, sem, n_rows):
    _row_copies(n_rows, lambda r: pltpu.make_async_copy(
        src.at[pl.ds(r, 1), :], dst_hbm.at[pl.ds(idx_ref[0, r], 1), :], sem))


def _wait_rows(src, dst, sem, n_rows):
    def wait(n):
        pltpu.make_async_copy(src.at[pl.ds(0, n), :], dst.at[pl.ds(0, n), :], sem).wait()

    if isinstance(n_rows, int):
        wait(n_rows)
        return
    full = pl.multiple_of(n_rows // SUBLANES * SUBLANES, SUBLANES)
    pl.when(full > 0)(functools.partial(wait, full))

    def body(r, carry):
        wait(1)
        return carry

    lax.fori_loop(full, n_rows, body, 0)


def _expert_kernel(te_ref, tf_ref, nv_ref, nu_ref, cur_ref, nxt_ref, dst_ref, h_hbm, wg_ref, wu_ref, wd_ref,
                   y_hbm, xbuf, ybuf, gsem, ssem, wgb, wub, wdb):
    del te_ref
    i = pl.program_id(0)
    n_used = nu_ref[0]
    tm = EXPERT_TILE
    slot = i % 2

    @pl.when(i == 0)
    def _():
        _gather_rows(cur_ref, h_hbm, xbuf.at[0], gsem.at[0], tm)

    @pl.when(i + 1 < n_used)
    def _():
        _gather_rows(nxt_ref, h_hbm, xbuf.at[1 - slot], gsem.at[1 - slot], tm)

    @pl.when(tf_ref[i] == 1)
    def _():
        wgb[...] = wg_ref[...].astype(BF16)
        wub[...] = wu_ref[...].astype(BF16)
        wdb[...] = wd_ref[...].astype(BF16)

    @pl.when(i < n_used)
    def _():
        @pl.when(i >= 2)
        def _():
            _wait_rows(ybuf.at[slot], y_hbm, ssem.at[slot], nv_ref[jnp.maximum(i - 2, 0)])

        _wait_rows(h_hbm, xbuf.at[slot], gsem.at[slot], tm)
        x = xbuf[slot].astype(BF16)
        a = (_silu(_dot(x, wgb[...])) * _dot(x, wub[...])).astype(BF16)
        ybuf[slot] = _dot(a, wdb[...])
        _scatter_rows(dst_ref, ybuf.at[slot], y_hbm, ssem.at[slot], nv_ref[i])

        @pl.when(i == n_used - 1)
        def _():
            @pl.when(i >= 1)
            def _():
                _wait_rows(ybuf.at[1 - slot], y_hbm, ssem.at[1 - slot], nv_ref[jnp.maximum(i - 1, 0)])

            _wait_rows(ybuf.at[slot], y_hbm, ssem.at[slot], nv_ref[i])


def _expert_call(h_all, plan, w_gate, w_up, w_down, layer):
    src_tok, dst_row, tile_expert, tile_first, tile_valid, n_used = plan
    n_tiles = tile_expert.shape[0]
    tm = EXPERT_TILE
    t_all, d = h_all.shape
    de = w_gate.shape[-1]
    src = src_tok.reshape(n_tiles, 1, tm)
    dst = dst_row.reshape(n_tiles, 1, tm)

    def last(i, nu):
        return jnp.minimum(i, nu[0] - 1)

    def w_map(i, te, tf, nv, nu):
        return (layer, te[last(i, nu)], 0, 0)

    def idx_spec(shift):
        return pl.BlockSpec((None, 1, tm), lambda i, te, tf, nv, nu: (last(i + shift, nu), 0, 0),
                            memory_space=pltpu.SMEM)

    grid_spec = pltpu.PrefetchScalarGridSpec(
        num_scalar_prefetch=4,
        grid=(n_tiles,),
        in_specs=[idx_spec(0), idx_spec(1), idx_spec(0),
                  pl.BlockSpec(memory_space=pl.ANY),
                  pl.BlockSpec((None, None, d, de), w_map),
                  pl.BlockSpec((None, None, d, de), w_map),
                  pl.BlockSpec((None, None, de, d), w_map)],
        out_specs=pl.BlockSpec(memory_space=pl.ANY),
        scratch_shapes=[pltpu.VMEM((2, tm, d), F32), pltpu.VMEM((2, tm, d), F32),
                        pltpu.SemaphoreType.DMA((2,)), pltpu.SemaphoreType.DMA((2,)),
                        pltpu.VMEM((d, de), BF16), pltpu.VMEM((d, de), BF16), pltpu.VMEM((de, d), BF16)],
    )
    return pl.pallas_call(
        _expert_kernel,
        grid_spec=grid_spec,
        out_shape=jax.ShapeDtypeStruct((TOP_K * t_all, d), F32),
        compiler_params=_cparams(1, disable_bounds_checks=True),
        name="routed_experts",
    )(tile_expert, tile_first, tile_valid, n_used, src, src, dst, h_all, w_gate, w_up, w_down)


def _combine_kernel(x_ref, h_ref, wgt_ref, gate_ref, sg_ref, su_ref, sd_ref, y_ref, o_ref):
    hb = h_ref[...].astype(BF16)
    a = (_silu(_dot(hb, sg_ref[...])) * _dot(hb, su_ref[...])).astype(BF16)
    moe = _dot(a, sd_ref[...])
    wgt = wgt_ref[...]
    for k in range(TOP_K):
        moe = moe + wgt[:, k:k + 1] * y_ref[k]
    o_ref[...] = x_ref[...] + gate_ref[...] * moe


def _combine_call(x, h, wgt, gate, ws_gate, ws_up, ws_down, y, row0, rows_per_mod, tc=128):
    m, d = x.shape
    tc = min(tc, m, rows_per_mod)
    assert row0 % tc == 0
    ds = ws_gate.shape[1]
    return pl.pallas_call(
        _combine_kernel,
        grid=(m // tc,),
        in_specs=[pl.BlockSpec((tc, d), lambda i: (i, 0)),
                  pl.BlockSpec((tc, d), lambda i: (i, 0)),
                  pl.BlockSpec((tc, TOP_K), lambda i: (i, 0)),
                  _mod_spec(gate, tc, rows_per_mod),
                  pl.BlockSpec((d, ds), lambda i: (0, 0)),
                  pl.BlockSpec((d, ds), lambda i: (0, 0)),
                  pl.BlockSpec((ds, d), lambda i: (0, 0)),
                  pl.BlockSpec((TOP_K, tc, d), lambda i: (0, row0 // tc + i, 0))],
        out_specs=pl.BlockSpec((tc, d), lambda i: (i, 0)),
        out_shape=jax.ShapeDtypeStruct((m, d), F32),
        compiler_params=_cparams(1),
        name="moe_combine_residual",
    )(x, h, wgt, gate, ws_gate, ws_up, ws_down, y)


def _table_lookup(table, idx):
    n = table.shape[0]
    return jnp.sum(jnp.where(idx[..., None] == jnp.arange(n, dtype=I32), table, 0), axis=-1)


def _dispatch_plan(eidx, counts):
    t, kk = eidx.shape
    tm = EXPERT_TILE
    n_pairs = t * kk
    n_tiles = (n_pairs + N_EXPERTS * (tm - 1)) // tm
    gsz = (counts + tm - 1) // tm * tm
    gend = jnp.cumsum(gsz)
    gstart = gend - gsz
    cstart = jnp.cumsum(counts) - counts
    bits = (n_pairs - 1).bit_length()
    pair = jnp.arange(n_pairs, dtype=I32).reshape(t, kk)
    order = jnp.sort(((eidx << bits) | pair).reshape(n_pairs)) & ((1 << bits) - 1)
    tile_start = jnp.arange(n_tiles, dtype=I32) * tm
    tile_expert = jnp.minimum(jnp.sum((gend[None, :] <= tile_start[:, None]).astype(I32), axis=1),
                              N_EXPERTS - 1)
    first = _table_lookup(cstart, tile_expert)
    cnt = _table_lookup(counts, tile_expert)
    last = first + jnp.maximum(cnt - 1, 0)
    off = tile_start - _table_lookup(gstart, tile_expert)
    j = jnp.clip((first + off)[:, None] + jnp.arange(tm, dtype=I32)[None, :], first[:, None], last[:, None])
    pair_of_slot = order[jnp.clip(j, 0, n_pairs - 1)]
    src_row = pair_of_slot // kk
    dst_row = (pair_of_slot % kk) * t + src_row
    tile_valid = jnp.clip(cnt - off, 0, tm)
    n_used = (gend[-1] // tm).astype(I32).reshape(1)
    prev = jnp.concatenate([jnp.full((1,), -1, I32), tile_expert[:-1]])
    tile_first = (tile_expert != prev).astype(I32)
    return (src_row.astype(I32), dst_row.astype(I32), tile_expert.astype(I32), tile_first,
            tile_valid.astype(I32), n_used)


def _moe_layer(xp, xs, mp, ms, g_ffn, w_router, b_router, w_gate, w_up, w_down, ws_gate, ws_up, ws_down, seq,
               layer):
    bd = xs.shape[0]
    cnt0 = jnp.zeros((1, N_EXPERTS), F32)
    hp, ep, wp, cnt_p = _router_call(xp, g_ffn, mp[4], mp[3], w_router, b_router, cnt0, seq)
    hs, es, ws, cnt = _router_call(xs, g_ffn, ms[4], ms[3], w_router, b_router, cnt_p, bd)
    h_all = jnp.concatenate([hp, hs], axis=0)
    t_all, d = h_all.shape
    plan = _dispatch_plan(jnp.concatenate([ep, es], axis=0), cnt.reshape(N_EXPERTS).astype(I32))
    y = _expert_call(h_all, plan, w_gate, w_up, w_down, layer).reshape(TOP_K, t_all, d)
    tp = xp.shape[0]
    shared = (ws_gate.astype(BF16), ws_up.astype(BF16), ws_down.astype(BF16))
    xp = _combine_call(xp, hp, wp, mp[5], *shared, y, 0, seq)
    xs = _combine_call(xs, hs, ws, ms[5], *shared, y, tp, bd)
    return xp, xs


def kernel(x_prompt, x_sample, cache_a_k, cache_a_v, cache_b_k, cache_b_v, cache_b_idx, page_table,
           c_prompt, c_sample, w_ada, b_ada, g_mix, g_ffn, a_w_in, a_q_norm, a_k_norm, a_lambda_q1,
           a_lambda_k1, a_lambda_q2, a_lambda_k2, a_subln, a_w_out, b_w_in, b_q_norm, b_k_norm, b_w_out,
           w_router, b_router, w_gate, w_up, w_down, ws_gate, ws_up, ws_down):
    batch, seq, d = x_prompt.shape
    bd, ts, _ = x_sample.shape
    assert ts == 1
    depth = w_ada.shape[0]
    page = cache_a_k.shape[2]
    past = page_table.shape[1] * page
    tp = batch * seq

    cos_p, sin_p = _rope_tables(jnp.arange(seq, dtype=jnp.int32), LANES)
    cos_s, sin_s = _rope_tables(past + jnp.arange(ts, dtype=jnp.int32), LANES)

    n_c = batch + bd
    pad = (-n_c) % SUBLANES
    c_all = jnp.concatenate([c_prompt, c_sample, jnp.zeros((pad, d), F32)], axis=0)
    mod = _mod_call(c_all, w_ada, b_ada).reshape(depth, n_c + pad, 6, d)

    xp = x_prompt.reshape(tp, d)
    xs = x_sample.reshape(bd, d)
    outs_a = [[], [], [], []]
    outs_b = [[], [], [], [], [], []]
    for i in range(depth):
        mp = [mod[i, :batch, t].reshape(batch, 1, d) for t in range(6)]
        ms = [mod[i, batch:n_c, t].reshape(1, bd, d) for t in range(6)]
        hp = _normmod_call(xp, g_mix[i], mp[1], mp[0], seq, min(512, seq))
        hs = _normmod_call(xs, g_mix[i], ms[1], ms[0], bd, bd)
        j = i // 2
        if i % 2 == 0:
            lam_init = 0.8 - 0.6 * math.exp(-0.3 * i)
            lams = (a_lambda_q1[j], a_lambda_k1[j], a_lambda_q2[j], a_lambda_k2[j])
            w_in = a_w_in[j]
            mix = A_HEADS * 2 * A_HEAD_DIM
            qp = _proj_call(hp, w_in, 0, mix, ("norm_rope",), gain=a_q_norm[j], rope=(cos_p, sin_p),
                            rope_rows=seq, out_f32=False, out_bf16=True)
            kp, kp_b = _proj_call(hp, w_in, mix, mix, ("norm_rope",), gain=a_k_norm[j], rope=(cos_p, sin_p),
                                  rope_rows=seq, out_bf16=True)
            vp, vp_b = _proj_call(hp, w_in, 2 * mix, mix, ("plain",), out_bf16=True)
            qs = _proj_call(hs, w_in, 0, mix, ("norm_rope",), gain=a_q_norm[j], rope=(cos_s, sin_s), rope_rows=1)
            ks = _proj_call(hs, w_in, mix, mix, ("norm_rope",), gain=a_k_norm[j], rope=(cos_s, sin_s), rope_rows=1)
            vs = _proj_call(hs, w_in, 2 * mix, mix, ("plain",))
            op = _diff_flash_call(qp, kp_b, vp_b, lams, a_subln[j], batch, seq, lam_init)
            hd = 2 * A_HEAD_DIM
            os_ = _diff_decode_call(qs.reshape(bd, A_HEADS, hd), ks.reshape(bd, A_HEADS, hd),
                                    vs.reshape(bd, A_HEADS, hd), cache_a_k[j], cache_a_v[j], page_table,
                                    lams, a_subln[j], lam_init).reshape(bd, mix)
            w_out = a_w_out[j]
            outs_a[0].append(kp.reshape(batch, seq, A_HEADS, hd))
            outs_a[1].append(vp.reshape(batch, seq, A_HEADS, hd))
            outs_a[2].append(ks.reshape(bd, ts, A_HEADS, hd))
            outs_a[3].append(vs.reshape(bd, ts, A_HEADS, hd))
        else:
            w_in = b_w_in[j]
            mix = B_HEADS * B_HEAD_DIM
            dh = B_HEAD_DIM
            o_kv = mix
            o_qi = mix + 2 * dh
            o_ki = o_qi + IDX_HEADS * IDX_DIM
            o_wi = o_ki + IDX_DIM
            w_wi = w_in[:, o_wi:o_wi + IDX_HEADS]
            wi_scale = IDX_HEADS ** -0.5

            def project(h, rope, rope_rows, bf16):
                q = _proj_call(h, w_in, 0, mix, ("norm_rope",), gain=b_q_norm[j], rope=rope, rope_rows=rope_rows,
                               out_f32=not bf16, out_bf16=bf16)
                kv = _proj_call(h, w_in, o_kv, 2 * dh, ("norm_rope", "plain"), gain=b_k_norm[j], rope=rope,
                                rope_rows=rope_rows, out_bf16=bf16, tn=2 * dh)
                qi = _proj_call(h, w_in, o_qi, IDX_HEADS * IDX_DIM, ("rope",), rope=rope, rope_rows=rope_rows,
                                out_f32=not bf16, out_bf16=bf16, tm=1024, tn=256)
                ki = _proj_call(h, w_in, o_ki, IDX_DIM, ("rope",), rope=rope, rope_rows=rope_rows,
                                out_bf16=bf16, tn=IDX_DIM)
                wi = _proj_call(h, w_wi, 0, IDX_HEADS, ("plain",), out_scale=wi_scale, tn=IDX_HEADS)
                return q, kv, qi, ki, wi

            qp, (kvp, kvp_b), qip, (kip, kip_b), wip = project(hp, (cos_p, sin_p), seq, True)
            qs, kvs, qis, kis, wis = project(hs, (cos_s, sin_s), 1, False)
            n_sel = min(IDX_TOPK_MAX, seq // 4)
            sc_p = _index_prompt_call(qip, wip, kip_b, batch, seq)
            thr_p = _kth_call(sc_p, n_sel, causal_seq=seq)
            op = _dsa_prompt_call(qp, kvp_b, sc_p, thr_p, batch, seq)
            sc_s, sc_new = _index_sample_call(qis.reshape(bd, IDX_HEADS, IDX_DIM), wis.reshape(bd, IDX_HEADS, 1),
                                              kis.reshape(bd, 1, IDX_DIM), cache_b_idx[j], page_table)
            n_sel_s = min(IDX_TOPK_MAX, (past + ts) // 4)
            thr_s = _kth_call(jnp.concatenate([sc_s.reshape(bd, past), sc_new.reshape(bd, LANES)], axis=1),
                              n_sel_s, tr=bd)
            os_ = _dsa_sample_call(qs.reshape(bd, B_HEADS, dh), kvs[:, :dh].reshape(bd, 1, dh),
                                   kvs[:, dh:].reshape(bd, 1, dh), sc_s, sc_new, thr_s.reshape(bd, 1, 1),
                                   cache_b_k[j], cache_b_v[j], page_table).reshape(bd, mix)
            w_out = b_w_out[j]
            outs_b[0].append(kvp[:, :dh].reshape(batch, seq, dh))
            outs_b[1].append(kvp[:, dh:].reshape(batch, seq, dh))
            outs_b[2].append(kip.reshape(batch, seq, IDX_DIM))
            outs_b[3].append(kvs[:, :dh].reshape(bd, ts, dh))
            outs_b[4].append(kvs[:, dh:].reshape(bd, ts, dh))
            outs_b[5].append(kis.reshape(bd, ts, IDX_DIM))
        xp = _resid_call(op, w_out, xp, mp[2], seq)
        xs = _resid_call(os_.astype(BF16), w_out, xs, ms[2], bd)
        xp, xs = _moe_layer(xp, xs, mp, ms, g_ffn[i], w_router[i], b_router[i], w_gate, w_up, w_down,
                            ws_gate[i], ws_up[i], ws_down[i], seq, i)
    return (xp.reshape(batch, seq, d), xs.reshape(bd, ts, d),
            *[jnp.stack(o) for o in outs_a], *[jnp.stack(o) for o in outs_b])
```

```python
import functools
import math

import jax
import jax.numpy as jnp
import numpy as np
from jax import lax
from jax.experimental import pallas as pl
from jax.experimental.pallas import tpu as pltpu

F32 = jnp.float32
BF16 = jnp.bfloat16
I32 = jnp.int32

A_HEADS = 8
A_HEAD_DIM = 128
B_HEADS = 16
B_HEAD_DIM = 128
IDX_HEADS = 16
IDX_DIM = 128
IDX_TOPK_MAX = 256
N_EXPERTS = 64
TOP_K = 8
N_GROUPS = 8
TOPK_GROUPS = 4
ROUTED_SCALE = 2.5
ROPE_THETA = 10000.0
EPS = 1e-6

LANES = 128
SUBLANES = 8
V7X_VMEM_LIMIT_BYTES = 56 * 1024 * 1024

NEG_INF = float("-inf")
LOG2E = math.log2(math.e)


def _cparams(n_grid, **kw):
    return pltpu.CompilerParams(dimension_semantics=("arbitrary",) * n_grid,
                                vmem_limit_bytes=V7X_VMEM_LIMIT_BYTES, **kw)


def _nt_dot(a, b):
    return lax.dot_general(a, b, (((1,), (1,)), ((), ())), preferred_element_type=F32)


def _dot(a, b):
    return jnp.dot(a, b, preferred_element_type=F32)


def _silu(x):
    return x * jax.nn.sigmoid(x)


def _mod_kernel(c_ref, w_ref, b_ref, o_ref):
    a_hi, a_lo = _split_bf16(_silu(c_ref[...]))
    w_hi, w_lo = _split_bf16(w_ref[...])
    o_ref[...] = _dot(a_hi, w_hi) + (_dot(a_hi, w_lo) + _dot(a_lo, w_hi)) + b_ref[...]


def _mod_call(c_all, w_ada, b_ada):
    n_layers, d, n = w_ada.shape
    r = c_all.shape[0]
    tn = 1024
    return pl.pallas_call(
        _mod_kernel,
        grid=(n_layers, n // tn),
        in_specs=[pl.BlockSpec((r, d), lambda l, j: (0, 0)),
                  pl.BlockSpec((None, d, tn), lambda l, j: (l, 0, j)),
                  pl.BlockSpec((None, 1, tn), lambda l, j: (l, 0, j))],
        out_specs=pl.BlockSpec((None, r, tn), lambda l, j: (l, 0, j)),
        out_shape=jax.ShapeDtypeStruct((n_layers, r, n), F32),
        compiler_params=_cparams(2),
        name="adaln_modulation",
    )(c_all, w_ada, b_ada.reshape(n_layers, 1, n))


def _normmod(x, g, scale, shift):
    y = x * lax.rsqrt(jnp.mean(x * x, axis=-1, keepdims=True) + EPS) * g
    return y * (1.0 + scale) + shift


def _normmod_kernel(x_ref, g_ref, sc_ref, sh_ref, o_ref):
    o_ref[...] = _normmod(x_ref[...], g_ref[...], sc_ref[...], sh_ref[...]).astype(o_ref.dtype)


def _mod_spec(mod, tm, rows_per_mod):
    _, r, d = mod.shape
    tiles = rows_per_mod // tm
    return pl.BlockSpec((None, r, d), lambda i: (i // tiles, 0, 0))


def _normmod_call(x, g, scale, shift, rows_per_mod, tm, out_dtype=BF16):
    m, d = x.shape
    return pl.pallas_call(
        _normmod_kernel,
        grid=(m // tm,),
        in_specs=[pl.BlockSpec((tm, d), lambda i: (i, 0)),
                  pl.BlockSpec((1, d), lambda i: (0, 0)),
                  _mod_spec(scale, tm, rows_per_mod),
                  _mod_spec(shift, tm, rows_per_mod)],
        out_specs=pl.BlockSpec((tm, d), lambda i: (i, 0)),
        out_shape=jax.ShapeDtypeStruct((m, d), out_dtype),
        compiler_params=_cparams(1),
        name="norm_modulate",
    )(x, g.reshape(1, d), scale, shift)


def _rope_tables(pos, d):
    inv = jnp.exp(-math.log(ROPE_THETA) * jnp.arange(0, d, 2, dtype=jnp.float32) / d)
    ang = pos.astype(jnp.float32)[:, None] * inv[None, :]
    cos, sin = jnp.cos(ang), jnp.sin(ang)
    return jnp.concatenate([cos, cos], axis=-1), jnp.concatenate([-sin, sin], axis=-1)


def _split_bf16(x):
    hi = x.astype(BF16)
    return hi, (x - hi.astype(F32)).astype(BF16)


def _cached_weight_dot(a_ref, w_ref, w_scratch):
    precise = len(w_scratch) == 2

    @pl.when(pl.program_id(1) == 0)
    def _():
        if precise:
            w_scratch[0][...], w_scratch[1][...] = _split_bf16(w_ref[...])
        else:
            w_scratch[0][...] = w_ref[...].astype(BF16)

    if not precise:
        return _dot(a_ref[...], w_scratch[0][...])
    a_hi, a_lo = _split_bf16(a_ref[...])
    return _dot(a_hi, w_scratch[0][...]) + (_dot(a_hi, w_scratch[1][...]) + _dot(a_lo, w_scratch[0][...]))


def _proj_kernel(*refs, kinds, has_gain, has_rope, out_f32, out_bf16, out_scale, precise):
    refs = list(refs)
    a_ref, w_ref = refs[0], refs[1]
    pos = 2
    gain_ref = cos_ref = sin_ref = None
    if has_gain:
        gain_ref = refs[pos]
        pos += 1
    if has_rope:
        cos_ref, sin_ref = refs[pos], refs[pos + 1]
        pos += 2
    n_scratch = 2 if precise else 1
    outs = refs[pos:-n_scratch]
    acc = _cached_weight_dot(a_ref, w_ref, refs[-n_scratch:])
    cw = acc.shape[1] // len(kinds)
    for c, kind in enumerate(kinds):
        sl = slice(c * cw, (c + 1) * cw)
        x = acc[:, sl]
        if kind == "norm_rope":
            x = x * lax.rsqrt(jnp.mean(x * x, axis=-1, keepdims=True) + EPS) * gain_ref[...]
        if kind in ("norm_rope", "rope"):
            x = x * cos_ref[...] + pltpu.roll(x, LANES // 2, 1) * sin_ref[...]
        if out_scale != 1.0:
            x = x * out_scale
        k = 0
        if out_f32:
            outs[k][:, sl] = x
            k += 1
        if out_bf16:
            outs[k][:, sl] = x.astype(BF16)


def _proj_call(a, w, col0, n, kinds_one, *, gain=None, rope=None, rope_rows=None, out_f32=True,
               out_bf16=False, out_scale=1.0, tm=512, tn=1024):
    m, k = a.shape
    tm = min(tm, m)
    if rope is not None and rope_rows > 1:
        tm = min(tm, rope_rows)
    tn = min(tn, n)
    cw = min(LANES, tn)
    assert m % tm == 0 and n % tn == 0 and col0 % tn == 0 and tn % cw == 0
    jb = col0 // tn
    kinds = tuple(kinds_one[(c * cw) // (n // len(kinds_one))] for c in range(n // cw))
    per_tile = [kinds[j * (tn // cw):(j + 1) * (tn // cw)] for j in range(n // tn)]
    assert all(p == per_tile[0] for p in per_tile)
    in_specs = [pl.BlockSpec((tm, k), lambda j, i: (i, 0)),
                pl.BlockSpec((k, tn), lambda j, i: (0, j + jb))]
    args = [a, w]
    if gain is not None:
        in_specs.append(pl.BlockSpec((1, LANES), lambda j, i: (0, 0)))
        args.append(gain.reshape(1, LANES))
    if rope is not None:
        if rope_rows == 1:
            spec = pl.BlockSpec((1, LANES), lambda j, i: (0, 0))
        else:
            tiles = rope_rows // tm
            spec = pl.BlockSpec((tm, LANES), lambda j, i: (i % tiles, 0))
        in_specs += [spec, spec]
        args += [rope[0], rope[1]]
    out_shape, out_specs = [], []
    for flag, dt in ((out_f32, F32), (out_bf16, BF16)):
        if flag:
            out_shape.append(jax.ShapeDtypeStruct((m, n), dt))
            out_specs.append(pl.BlockSpec((tm, tn), lambda j, i: (i, j)))
    res = pl.pallas_call(
        functools.partial(_proj_kernel, kinds=per_tile[0], has_gain=gain is not None,
                          has_rope=rope is not None, out_f32=out_f32, out_bf16=out_bf16,
                          out_scale=out_scale, precise=a.dtype == F32),
        grid=(n // tn, m // tm),
        in_specs=in_specs,
        out_specs=out_specs,
        out_shape=out_shape,
        scratch_shapes=[pltpu.VMEM((k, tn), BF16)] * (2 if a.dtype == F32 else 1),
        compiler_params=_cparams(2),
        name="projection",
    )(*args)
    return res if len(res) > 1 else res[0]


def _resid_kernel(a_ref, w_ref, x_ref, gate_ref, o_ref, *w_scratch):
    o_ref[...] = x_ref[...] + gate_ref[...] * _cached_weight_dot(a_ref, w_ref, w_scratch)


def _resid_call(a, w, x, gate, rows_per_mod, tm=512, tn=1024):
    m, k = a.shape
    n = w.shape[1]
    tm = min(tm, m, rows_per_mod)
    tiles = rows_per_mod // tm
    r = gate.shape[1]
    return pl.pallas_call(
        _resid_kernel,
        grid=(n // tn, m // tm),
        in_specs=[pl.BlockSpec((tm, k), lambda j, i: (i, 0)),
                  pl.BlockSpec((k, tn), lambda j, i: (0, j)),
                  pl.BlockSpec((tm, tn), lambda j, i: (i, j)),
                  pl.BlockSpec((None, r, tn), lambda j, i: (i // tiles, 0, j))],
        out_specs=pl.BlockSpec((tm, tn), lambda j, i: (i, j)),
        out_shape=jax.ShapeDtypeStruct((m, n), F32),
        scratch_shapes=[pltpu.VMEM((k, tn), BF16)] * (2 if a.dtype == F32 else 1),
        compiler_params=_cparams(2),
        name="out_projection_residual",
    )(a, w, x, gate)


def _diff_lambda(lq1_ref, lk1_ref, lq2_ref, lk2_ref, lam_init):
    s1 = jnp.sum(lq1_ref[...] * lk1_ref[...], axis=-1, keepdims=True)
    s2 = jnp.sum(lq2_ref[...] * lk2_ref[...], axis=-1, keepdims=True)
    return jnp.exp(s1) - jnp.exp(s2) + lam_init


def _subln(o, subln, lam_init):
    o = o * lax.rsqrt(jnp.mean(o * o, axis=-1, keepdims=True) + EPS) * subln
    return o * (1.0 - lam_init)


def _diff_flash_kernel(q_ref, k_ref, v_ref, lq1_ref, lk1_ref, lq2_ref, lk2_ref, subln_ref, o_ref,
                       m_ref, l_ref, acc_ref, *, tq, lam_init):
    qi = pl.program_id(2)
    ki = pl.program_id(3)
    d = A_HEAD_DIM

    @pl.when(ki == 0)
    def _():
        m_ref[...] = jnp.full(m_ref.shape, NEG_INF, F32)
        l_ref[...] = jnp.zeros(l_ref.shape, F32)
        acc_ref[...] = jnp.zeros(acc_ref.shape, F32)

    def update(diagonal):
        q = q_ref[...]
        k = k_ref[...]
        v = v_ref[...]
        for c in range(2):
            s = _nt_dot(q[:, c * d:(c + 1) * d], k[:, c * d:(c + 1) * d])
            if diagonal:
                keep = lax.broadcasted_iota(I32, (tq, tq), 1) <= lax.broadcasted_iota(I32, (tq, tq), 0)
                s = jnp.where(keep, s, NEG_INF)
            m_prev = m_ref[c]
            m_new = jnp.maximum(m_prev, jnp.max(s, axis=-1, keepdims=True))
            alpha = jnp.exp2(m_prev - m_new)
            p = jnp.exp2(s - m_new)
            l_ref[c] = alpha * l_ref[c] + jnp.sum(p, axis=-1, keepdims=True)
            acc_ref[c] = alpha * acc_ref[c] + _dot(p.astype(BF16), v)
            m_ref[c] = m_new

    pl.when(ki < qi)(functools.partial(update, False))
    pl.when(ki == qi)(functools.partial(update, True))

    @pl.when(ki == pl.num_programs(3) - 1)
    def _():
        lam = _diff_lambda(lq1_ref, lk1_ref, lq2_ref, lk2_ref, lam_init)
        o = acc_ref[0] / l_ref[0] - lam * (acc_ref[1] / l_ref[1])
        o_ref[...] = _subln(o, subln_ref[...], lam_init).astype(o_ref.dtype)


def _diff_flash_call(q, k, v, lams, subln, batch, seq, lam_init, tq=512):
    tq = min(tq, seq)
    nq = seq // tq
    hd = 2 * A_HEAD_DIM
    vec = pl.BlockSpec((1, A_HEAD_DIM), lambda b, h, i, j: (0, 0))
    return pl.pallas_call(
        functools.partial(_diff_flash_kernel, tq=tq, lam_init=lam_init),
        grid=(batch, A_HEADS, nq, nq),
        in_specs=[pl.BlockSpec((tq, hd), lambda b, h, i, j: (b * nq + i, h)),
                  pl.BlockSpec((tq, hd), lambda b, h, i, j: (b * nq + jnp.minimum(i, j), h)),
                  pl.BlockSpec((tq, hd), lambda b, h, i, j: (b * nq + jnp.minimum(i, j), h)),
                  vec, vec, vec, vec,
                  pl.BlockSpec((1, hd), lambda b, h, i, j: (0, 0))],
        out_specs=pl.BlockSpec((tq, hd), lambda b, h, i, j: (b * nq + i, h)),
        out_shape=jax.ShapeDtypeStruct(q.shape, BF16),
        scratch_shapes=[pltpu.VMEM((2, tq, 1), F32), pltpu.VMEM((2, tq, 1), F32),
                        pltpu.VMEM((2, tq, hd), F32)],
        compiler_params=_cparams(4),
        name="diff_attention_prompt",
    )(q, k, v, *[x.reshape(1, A_HEAD_DIM) for x in lams], subln.reshape(1, hd))


def _diff_decode_kernel(pt_ref, q_ref, kn_ref, vn_ref, lq1_ref, lk1_ref, lq2_ref, lk2_ref, subln_ref,
                        *rest, pages, page, lam_init):
    del pt_ref
    k_refs = rest[:pages]
    v_refs = rest[pages:2 * pages]
    o_ref = rest[2 * pages]
    m_ref, l_ref, acc_ref = rest[2 * pages + 1:]
    step = pl.program_id(1)
    d = A_HEAD_DIM
    h = A_HEADS
    scale = d ** -0.5
    q = q_ref[...]
    lane = lax.broadcasted_iota(I32, (h, 2 * d), 1)
    qbd = jnp.concatenate([jnp.where(lane < d, q, 0.0), jnp.where(lane >= d, q, 0.0)], axis=0)

    @pl.when(step == 0)
    def _():
        prod = q * kn_ref[...]
        s0 = jnp.sum(jnp.where(lane < d, prod, 0.0), axis=-1, keepdims=True)
        s1 = jnp.sum(jnp.where(lane >= d, prod, 0.0), axis=-1, keepdims=True)
        m_ref[...] = jnp.concatenate([s0, s1], axis=0) * scale
        l_ref[...] = jnp.ones(l_ref.shape, F32)
        acc_ref[...] = jnp.concatenate([vn_ref[...], vn_ref[...]], axis=0)

    n = page * h
    own = (lax.broadcasted_iota(I32, (2 * h, n), 1) % h) == (lax.broadcasted_iota(I32, (2 * h, n), 0) % h)
    q2 = jnp.concatenate(_split_bf16(qbd), axis=0)
    s_list = []
    for r in range(pages):
        kf = k_refs[r][...].reshape(n, 2 * d).astype(BF16)
        s2 = _nt_dot(q2, kf)
        s = (s2[:2 * h] + s2[2 * h:]) * scale
        s_list.append(jnp.where(own, s, NEG_INF))
    m_prev = m_ref[...]
    m_new = m_prev
    for s in s_list:
        m_new = jnp.maximum(m_new, jnp.max(s, axis=-1, keepdims=True))
    alpha = jnp.exp(m_prev - m_new)
    l_new = alpha * l_ref[...]
    acc = alpha * acc_ref[...]
    for r in range(pages):
        p = jnp.exp(s_list[r] - m_new)
        l_new = l_new + jnp.sum(p, axis=-1, keepdims=True)
        vf = v_refs[r][...].reshape(n, 2 * d).astype(BF16)
        o2 = _dot(jnp.concatenate(_split_bf16(p), axis=0), vf)
        acc = acc + (o2[:2 * h] + o2[2 * h:])
    m_ref[...] = m_new
    l_ref[...] = l_new
    acc_ref[...] = acc

    @pl.when(step == pl.num_programs(1) - 1)
    def _():
        lam = _diff_lambda(lq1_ref, lk1_ref, lq2_ref, lk2_ref, lam_init)
        o = acc[:h] / l_new[:h] - lam * (acc[h:] / l_new[h:])
        o_ref[...] = _subln(o, subln_ref[...], lam_init).astype(o_ref.dtype)


def _diff_decode_call(q, k_new, v_new, cache_k, cache_v, page_table, lams, subln, lam_init, pages=8):
    bd, n_pages = page_table.shape
    _, page, h, hd = cache_k.shape
    pages = min(pages, n_pages)
    assert n_pages % pages == 0
    row = pl.BlockSpec((None, h, hd), lambda b, s, pt: (b, 0, 0))
    vec = pl.BlockSpec((1, A_HEAD_DIM), lambda b, s, pt: (0, 0))

    def page_spec(r):
        return pl.BlockSpec((None, page, h, hd), lambda b, s, pt: (pt[b, s * pages + r], 0, 0, 0))

    grid_spec = pltpu.PrefetchScalarGridSpec(
        num_scalar_prefetch=1,
        grid=(bd, n_pages // pages),
        in_specs=[row, row, row, vec, vec, vec, vec, pl.BlockSpec((1, hd), lambda b, s, pt: (0, 0))]
        + [page_spec(r) for r in range(pages)] + [page_spec(r) for r in range(pages)],
        out_specs=row,
        scratch_shapes=[pltpu.VMEM((2 * h, 1), F32), pltpu.VMEM((2 * h, 1), F32),
                        pltpu.VMEM((2 * h, hd), F32)],
    )
    return pl.pallas_call(
        functools.partial(_diff_decode_kernel, pages=pages, page=page, lam_init=lam_init),
        grid_spec=grid_spec,
        out_shape=jax.ShapeDtypeStruct((bd, h, hd), F32),
        compiler_params=_cparams(2),
        name="diff_attention_sample",
    )(page_table, q, k_new, v_new, *[x.reshape(1, A_HEAD_DIM) for x in lams], subln.reshape(1, hd),
      *([cache_k] * pages), *([cache_v] * pages))


def _index_prompt_kernel(qi_ref, wi_ref, ki_ref, o_ref, *, tq):
    i = pl.program_id(1)
    j = pl.program_id(2)

    @pl.when(j > i)
    def _():
        o_ref[...] = jnp.full(o_ref.shape, NEG_INF, F32)

    @pl.when(j <= i)
    def _():
        ki = ki_ref[...]
        wi = wi_ref[...]
        acc = jnp.zeros((tq, tq), F32)
        for h in range(IDX_HEADS):
            s = _nt_dot(qi_ref[:, h * IDX_DIM:(h + 1) * IDX_DIM], ki)
            acc = acc + jnp.maximum(s, 0.0) * wi[:, h:h + 1]
        acc = acc * (IDX_DIM ** -0.5)
        row = i * tq + lax.broadcasted_iota(I32, (tq, tq), 0)
        col = j * tq + lax.broadcasted_iota(I32, (tq, tq), 1)
        o_ref[...] = jnp.where(col <= row, acc, NEG_INF)


def _index_prompt_call(qi, wi, ki, batch, seq, tq=512):
    tq = min(tq, seq)
    nq = seq // tq
    return pl.pallas_call(
        functools.partial(_index_prompt_kernel, tq=tq),
        grid=(batch, nq, nq),
        in_specs=[pl.BlockSpec((tq, IDX_HEADS * IDX_DIM), lambda b, i, j: (b * nq + i, 0)),
                  pl.BlockSpec((tq, IDX_HEADS), lambda b, i, j: (b * nq + i, 0)),
                  pl.BlockSpec((tq, IDX_DIM), lambda b, i, j: (b * nq + jnp.minimum(i, j), 0))],
        out_specs=pl.BlockSpec((tq, tq), lambda b, i, j: (b * nq + i, j)),
        out_shape=jax.ShapeDtypeStruct((batch * seq, seq), F32),
        compiler_params=_cparams(3),
        name="dsa_index_scores_prompt",
    )(qi, wi, ki)


def _float_key(x):
    b = pltpu.bitcast(x, I32)
    return jnp.where(b < 0, b ^ jnp.int32(0x7FFFFFFF), b)


def _causal_widths(tile, tq, seq, chunk, fn):
    chunk = min(chunk, seq)
    need = ((tile + 1) * tq + chunk - 1) // chunk
    for v in range(1, seq // chunk + 1):
        pl.when(need == v)(functools.partial(fn, v * chunk))


def _kth_kernel(x_ref, o_ref, *, kth, causal_seq):
    rows, cols = x_ref.shape
    int_min = jnp.int32(-2 ** 31)

    def search(width):
        key = _float_key(x_ref[:, :width])
        key_ninf = jnp.int32(-2 ** 31 + 0x7FFFFF)
        extra = float(cols - width)

        def count_ge(t):
            c = jnp.sum(jnp.where(key >= t, 1.0, 0.0), axis=-1, keepdims=True)
            return c + jnp.where(t <= key_ninf, extra, 0.0) if extra else c

        t0 = jnp.where(count_ge(jnp.zeros((rows, 1), I32)) >= kth, jnp.int32(0), int_min)

        def body(b, t):
            cand = t + (jnp.int32(1) << (jnp.int32(30) - b))
            return jnp.where(count_ge(cand) >= kth, cand, t)

        t = lax.fori_loop(0, 31, body, t0)
        bits = jnp.where(t < 0, t ^ jnp.int32(0x7FFFFFFF), t)
        o_ref[...] = pltpu.bitcast(bits, F32)

    if causal_seq is None:
        search(cols)
    else:
        _causal_widths(pl.program_id(0) % (causal_seq // rows), rows, causal_seq, 512, search)


def _kth_call(x, kth, tr=256, causal_seq=None):
    r, c = x.shape
    tr = min(tr, r)
    return pl.pallas_call(
        functools.partial(_kth_kernel, kth=float(kth), causal_seq=causal_seq),
        grid=(r // tr,),
        in_specs=[pl.BlockSpec((tr, c), lambda i: (i, 0))],
        out_specs=pl.BlockSpec((tr, 1), lambda i: (i, 0)),
        out_shape=jax.ShapeDtypeStruct((r, 1), F32),
        compiler_params=_cparams(1),
        name="kth_largest",
    )(x)


def _dsa_prompt_kernel(q_ref, k_ref, v_ref, sc_ref, thr_ref, o_ref):
    d = B_HEAD_DIM
    scale = d ** -0.5
    tq, seq = sc_ref.shape

    def attend(width):
        sc = sc_ref[:, :width]
        sel = jnp.logical_and(sc >= thr_ref[...], sc > NEG_INF)
        s = _nt_dot(q_ref[...], k_ref[:width, :]) * scale
        s = jnp.where(sel, s, NEG_INF)
        m = jnp.max(s, axis=-1, keepdims=True)
        p = jnp.exp(s - m)
        l = jnp.sum(p, axis=-1, keepdims=True)
        o_ref[...] = (_dot(p.astype(BF16), v_ref[:width, :]) / l).astype(o_ref.dtype)

    _causal_widths(pl.program_id(1), tq, seq, 512, attend)


def _dsa_prompt_call(q, kv, scores, thr, batch, seq, tq=512):
    tq = min(tq, seq)
    nq = seq // tq
    d = B_HEAD_DIM
    return pl.pallas_call(
        _dsa_prompt_kernel,
        grid=(batch, nq, B_HEADS),
        in_specs=[pl.BlockSpec((tq, d), lambda b, i, h: (b * nq + i, h)),
                  pl.BlockSpec((seq, d), lambda b, i, h: (b, 0)),
                  pl.BlockSpec((seq, d), lambda b, i, h: (b, 1)),
                  pl.BlockSpec((tq, seq), lambda b, i, h: (b * nq + i, 0)),
                  pl.BlockSpec((tq, 1), lambda b, i, h: (b * nq + i, 0))],
        out_specs=pl.BlockSpec((tq, d), lambda b, i, h: (b * nq + i, h)),
        out_shape=jax.ShapeDtypeStruct(q.shape, BF16),
        compiler_params=_cparams(3),
        name="dsa_attention_prompt",
    )(q, kv, kv, scores, thr)


def _index_sample_kernel(pt_ref, qi_ref, wi_ref, kin_ref, *rest, pages, page):
    del pt_ref
    pg_refs = rest[:pages]
    o_ref, on_ref = rest[pages], rest[pages + 1]
    qi = qi_ref[...]
    wi = wi_ref[...]
    del page
    keys = jnp.concatenate([pg_refs[r][...].astype(BF16) for r in range(pages)], axis=0)
    s = _nt_dot(qi.astype(BF16), keys)
    o_ref[...] = jnp.sum(jnp.maximum(s, 0.0) * wi, axis=0, keepdims=True) * (IDX_DIM ** -0.5)

    @pl.when(pl.program_id(1) == 0)
    def _():
        s = jnp.sum(qi * kin_ref[...], axis=-1, keepdims=True)
        sn = jnp.sum(jnp.maximum(s, 0.0) * wi, axis=0, keepdims=True) * (IDX_DIM ** -0.5)
        lane = lax.broadcasted_iota(I32, (1, LANES), 1)
        on_ref[...] = jnp.where(lane == 0, sn, NEG_INF)


def _index_sample_call(qi, wi, ki_new, cache_idx, page_table, pages=16):
    bd, n_pages = page_table.shape
    _, page, d = cache_idx.shape
    pages = min(pages, n_pages)
    assert n_pages % pages == 0

    def page_spec(r):
        return pl.BlockSpec((None, page, d), lambda b, s, pt: (pt[b, s * pages + r], 0, 0))

    grid_spec = pltpu.PrefetchScalarGridSpec(
        num_scalar_prefetch=1,
        grid=(bd, n_pages // pages),
        in_specs=[pl.BlockSpec((None, IDX_HEADS, d), lambda b, s, pt: (b, 0, 0)),
                  pl.BlockSpec((None, IDX_HEADS, 1), lambda b, s, pt: (b, 0, 0)),
                  pl.BlockSpec((None, 1, d), lambda b, s, pt: (b, 0, 0))]
        + [page_spec(r) for r in range(pages)],
        out_specs=[pl.BlockSpec((None, 1, pages * page), lambda b, s, pt: (b, 0, s)),
                   pl.BlockSpec((None, 1, LANES), lambda b, s, pt: (b, 0, 0))],
    )
    return pl.pallas_call(
        functools.partial(_index_sample_kernel, pages=pages, page=page),
        grid_spec=grid_spec,
        out_shape=[jax.ShapeDtypeStruct((bd, 1, n_pages * page), F32),
                   jax.ShapeDtypeStruct((bd, 1, LANES), F32)],
        compiler_params=_cparams(2),
        name="dsa_index_scores_sample",
    )(page_table, qi, wi, ki_new, *([cache_idx] * pages))


def _dsa_sample_kernel(pt_ref, q_ref, kn_ref, vn_ref, sc_ref, scn_ref, thr_ref, *rest, pages, page):
    del pt_ref
    k_refs = rest[:pages]
    v_refs = rest[pages:2 * pages]
    o_ref = rest[2 * pages]
    m_ref, l_ref, acc_ref = rest[2 * pages + 1:]
    d = B_HEAD_DIM
    scale = d ** -0.5
    q = q_ref[...]
    thr = thr_ref[...]

    @pl.when(pl.program_id(1) == 0)
    def _():
        s = jnp.sum(q * kn_ref[...], axis=-1, keepdims=True) * scale
        scn = scn_ref[...][:, 0:1]
        sel = jnp.logical_and(scn >= thr, scn > NEG_INF)
        m_ref[...] = jnp.where(sel, s, -1e30)
        l_ref[...] = jnp.where(sel, 1.0, 0.0) * jnp.ones(l_ref.shape, F32)
        acc_ref[...] = jnp.where(sel, 1.0, 0.0) * (jnp.ones((B_HEADS, 1), F32) * vn_ref[...])

    del page
    keys = jnp.concatenate([k_refs[r][...].astype(BF16) for r in range(pages)], axis=0)
    vals = jnp.concatenate([v_refs[r][...].astype(BF16) for r in range(pages)], axis=0)
    sc = sc_ref[...]
    sel = jnp.logical_and(sc >= thr, sc > NEG_INF)
    s = jnp.where(sel, _nt_dot(q.astype(BF16), keys) * scale, -1e30)
    m_run = m_ref[...]
    m_new = jnp.maximum(m_run, jnp.max(s, axis=-1, keepdims=True))
    alpha = jnp.exp(m_run - m_new)
    p = jnp.where(sel, jnp.exp(s - m_new), 0.0)
    l_run = alpha * l_ref[...] + jnp.sum(p, axis=-1, keepdims=True)
    acc = alpha * acc_ref[...] + _dot(p.astype(BF16), vals)
    m_ref[...] = m_new
    l_ref[...] = l_run
    acc_ref[...] = acc

    @pl.when(pl.program_id(1) == pl.num_programs(1) - 1)
    def _():
        o_ref[...] = (acc / l_run).astype(o_ref.dtype)


def _dsa_sample_call(q, k_new, v_new, scores, score_new, thr, cache_k, cache_v, page_table, pages=16):
    bd, n_pages = page_table.shape
    _, page, d = cache_k.shape
    pages = min(pages, n_pages)
    assert n_pages % pages == 0

    def page_spec(r):
        return pl.BlockSpec((None, page, d), lambda b, s, pt: (pt[b, s * pages + r], 0, 0))

    one = pl.BlockSpec((None, 1, d), lambda b, s, pt: (b, 0, 0))
    grid_spec = pltpu.PrefetchScalarGridSpec(
        num_scalar_prefetch=1,
        grid=(bd, n_pages // pages),
        in_specs=[pl.BlockSpec((None, B_HEADS, d), lambda b, s, pt: (b, 0, 0)), one, one,
                  pl.BlockSpec((None, 1, pages * page), lambda b, s, pt: (b, 0, s)),
                  pl.BlockSpec((None, 1, LANES), lambda b, s, pt: (b, 0, 0)),
                  pl.BlockSpec((None, 1, 1), lambda b, s, pt: (b, 0, 0))]
        + [page_spec(r) for r in range(pages)] + [page_spec(r) for r in range(pages)],
        out_specs=pl.BlockSpec((None, B_HEADS, d), lambda b, s, pt: (b, 0, 0)),
        scratch_shapes=[pltpu.VMEM((B_HEADS, 1), F32), pltpu.VMEM((B_HEADS, 1), F32),
                        pltpu.VMEM((B_HEADS, d), F32)],
    )
    return pl.pallas_call(
        functools.partial(_dsa_sample_kernel, pages=pages, page=page),
        grid_spec=grid_spec,
        out_shape=jax.ShapeDtypeStruct((bd, B_HEADS, d), F32),
        compiler_params=_cparams(2),
        name="dsa_attention_sample",
    )(page_table, q, k_new, v_new, scores, score_new, thr, *([cache_k] * pages), *([cache_v] * pages))


def _first_argmax(v, lane, width):
    m = jnp.max(v, axis=-1, keepdims=True)
    idx = jnp.min(jnp.where(v == m, lane, width), axis=-1, keepdims=True)
    return m, idx


def _to_row_major_tiles(ref, lead, x):
    rows, d = x.shape
    per = d // LANES
    for c in range(per):
        ref[(*lead, pl.ds(c, rows, stride=per), slice(None))] = x[:, c * LANES:(c + 1) * LANES]


def _from_row_major_tiles(ref, lead, rows, d):
    per = d // LANES
    return [ref[(*lead, pl.ds(c, rows, stride=per), slice(None))] for c in range(per)]


def _router_kernel(x_ref, g_ref, sc_ref, sh_ref, wr_ref, br_ref, cnt0_ref, ht_ref, hb_ref, eidx_ref, wgt_ref,
                   cnt_ref, cnt_sc):
    @pl.when(pl.program_id(0) == 0)
    def _():
        cnt_sc[...] = cnt0_ref[...]

    h = _normmod(x_ref[...], g_ref[...], sc_ref[...], sh_ref[...])
    _to_row_major_tiles(ht_ref, (), h)
    w = wr_ref[...]
    h_hi = h.astype(BF16)
    hb_ref[...] = h_hi
    h_lo = (h - h_hi.astype(F32)).astype(BF16)
    w_hi = w.astype(BF16)
    w_lo = (w - w_hi.astype(F32)).astype(BF16)
    logits = _dot(h_hi, w_hi) + (_dot(h_hi, w_lo) + _dot(h_lo, w_hi))
    scores = jax.nn.sigmoid(logits)
    biased = scores + br_ref[...]
    tm = scores.shape[0]
    lane = lax.broadcasted_iota(I32, (tm, N_EXPERTS), 1)
    per = N_EXPERTS // N_GROUPS
    grp = lane // per
    gs = []
    for g in range(N_GROUPS):
        vg = jnp.where(grp == g, biased, NEG_INF)
        m1, i1 = _first_argmax(vg, lane, N_EXPERTS)
        m2 = jnp.max(jnp.where(lane == i1, NEG_INF, vg), axis=-1, keepdims=True)
        gs.append(m1 + m2)
    emask = jnp.zeros((tm, N_EXPERTS), jnp.bool_)
    for g in range(N_GROUPS):
        beat = jnp.zeros((tm, 1), F32)
        for o in range(N_GROUPS):
            if o == g:
                continue
            wins = (gs[o] >= gs[g]) if o < g else (gs[o] > gs[g])
            beat = beat + jnp.where(wins, 1.0, 0.0)
        emask = jnp.logical_or(emask, jnp.logical_and(grp == g, beat < TOPK_GROUPS))
    masked = jnp.where(emask, biased, NEG_INF)
    lane_k = lax.broadcasted_iota(I32, (tm, TOP_K), 1)
    eidx = jnp.zeros((tm, TOP_K), I32)
    wsel = jnp.zeros((tm, TOP_K), F32)
    chosen = jnp.zeros((tm, N_EXPERTS), F32)
    for it in range(TOP_K):
        _, i = _first_argmax(masked, lane, N_EXPERTS)
        hit = lane == i
        chosen = chosen + jnp.where(hit, 1.0, 0.0)
        wv = jnp.sum(jnp.where(hit, scores, 0.0), axis=-1, keepdims=True)
        eidx = jnp.where(lane_k == it, i, eidx)
        wsel = jnp.where(lane_k == it, wv, wsel)
        masked = jnp.where(hit, NEG_INF, masked)
    eidx_ref[...] = eidx
    wgt_ref[...] = wsel / jnp.sum(wsel, axis=-1, keepdims=True) * ROUTED_SCALE
    cnt_sc[...] = cnt_sc[...] + jnp.sum(chosen, axis=0, keepdims=True)
    cnt_ref[...] = cnt_sc[...]


def _router_call(x, g, scale, shift, w_router, b_router, cnt0, rows_per_mod, tm=256):
    m, d = x.shape
    tm = min(tm, m, rows_per_mod)
    return pl.pallas_call(
        _router_kernel,
        grid=(m // tm,),
        in_specs=[pl.BlockSpec((tm, d), lambda i: (i, 0)),
                  pl.BlockSpec((1, d), lambda i: (0, 0)),
                  _mod_spec(scale, tm, rows_per_mod),
                  _mod_spec(shift, tm, rows_per_mod),
                  pl.BlockSpec((d, N_EXPERTS), lambda i: (0, 0)),
                  pl.BlockSpec((1, N_EXPERTS), lambda i: (0, 0)),
                  pl.BlockSpec((1, N_EXPERTS), lambda i: (0, 0))],
        out_specs=[pl.BlockSpec((tm * (d // LANES), LANES), lambda i: (i, 0)),
                   pl.BlockSpec((tm, d), lambda i: (i, 0)),
                   pl.BlockSpec((tm, TOP_K), lambda i: (i, 0)),
                   pl.BlockSpec((tm, TOP_K), lambda i: (i, 0)),
                   pl.BlockSpec((1, N_EXPERTS), lambda i: (0, 0))],
        out_shape=[jax.ShapeDtypeStruct((m * (d // LANES), LANES), F32),
                   jax.ShapeDtypeStruct((m, d), BF16),
                   jax.ShapeDtypeStruct((m, TOP_K), I32),
                   jax.ShapeDtypeStruct((m, TOP_K), F32),
                   jax.ShapeDtypeStruct((1, N_EXPERTS), F32)],
        scratch_shapes=[pltpu.VMEM((1, N_EXPERTS), F32)],
        compiler_params=_cparams(1),
        name="ffn_norm_router",
    )(x, g.reshape(1, d), scale, shift, w_router, b_router.reshape(1, N_EXPERTS), cnt0)


EXPERT_TILE = 256


ROW_BLOCK = 16


def _row_copy(src, src_row, dst, dst_row, sem):
    return pltpu.make_async_copy(src.at[pl.ds(pl.multiple_of(src_row, ROW_BLOCK), ROW_BLOCK), :],
                                 dst.at[pl.ds(pl.multiple_of(dst_row, ROW_BLOCK), ROW_BLOCK), :], sem)


def _wait_rows(src, dst, sem, n_blocks):
    n = pl.multiple_of(n_blocks * ROW_BLOCK, ROW_BLOCK)
    pltpu.make_async_copy(src.at[pl.ds(0, n), :], dst.at[pl.ds(0, n), :], sem).wait()


def _expert_kernel(te_ref, tf_ref, nv_ref, nu_ref, cur_ref, nxt_ref, dprev_ref, dcur_ref, h_hbm, wg_ref, wu_ref,
                   wd_ref, y_hbm, xbuf, ybuf, xb, gsem, ssem, wgb, wub, wdb):
    del te_ref
    i = pl.program_id(0)
    n_used = nu_ref[0]
    tm = EXPERT_TILE
    xs = i % 2
    ys = i % 3
    d = xb.shape[1]

    @pl.when(i == 0)
    def _():
        def body(r, carry):
            _row_copy(h_hbm, cur_ref[0, r], xbuf.at[0], r * ROW_BLOCK, gsem.at[0]).start()
            return carry

        lax.fori_loop(0, tm, body, 0)

    @pl.when(tf_ref[i] == 1)
    def _():
        wgb[...] = wg_ref[...].astype(BF16)
        wub[...] = wu_ref[...].astype(BF16)
        wdb[...] = wd_ref[...].astype(BF16)

    @pl.when(i < n_used)
    def _():
        _wait_rows(h_hbm, xbuf.at[xs], gsem.at[xs], tm)

        @pl.when(i >= 3)
        def _():
            _wait_rows(ybuf.at[ys], y_hbm, ssem.at[ys], nv_ref[jnp.maximum(i - 3, 0)])

        for c, chunk in enumerate(_from_row_major_tiles(xbuf, (xs,), tm, d)):
            xb[:, c * LANES:(c + 1) * LANES] = chunk.astype(BF16)

        has_next = i + 1 < n_used
        for r in range(tm):
            pl.when(has_next)(_row_copy(h_hbm, nxt_ref[0, r], xbuf.at[1 - xs], r * ROW_BLOCK, gsem.at[1 - xs]).start)
        n_prev = jnp.where(i >= 1, nv_ref[jnp.maximum(i - 1, 0)], 0)
        yp = (i + 2) % 3
        for r in range(tm):
            pl.when(r < n_prev)(_row_copy(ybuf.at[yp], r * ROW_BLOCK, y_hbm, dprev_ref[0, r], ssem.at[yp]).start)

        x = xb[...]
        a = (_silu(_dot(x, wgb[...])) * _dot(x, wub[...])).astype(BF16)
        _to_row_major_tiles(ybuf, (ys,), _dot(a, wdb[...]))

        @pl.when(i == n_used - 1)
        def _():
            def body(r, carry):
                _row_copy(ybuf.at[ys], r * ROW_BLOCK, y_hbm, dcur_ref[0, r], ssem.at[ys]).start()
                return carry

            lax.fori_loop(0, nv_ref[i], body, 0)

            @pl.when(i >= 2)
            def _():
                _wait_rows(ybuf.at[(i + 1) % 3], y_hbm, ssem.at[(i + 1) % 3], nv_ref[jnp.maximum(i - 2, 0)])

            @pl.when(i >= 1)
            def _():
                _wait_rows(ybuf.at[yp], y_hbm, ssem.at[yp], nv_ref[jnp.maximum(i - 1, 0)])

            _wait_rows(ybuf.at[ys], y_hbm, ssem.at[ys], nv_ref[i])


def _expert_call(h_all, plan, w_gate, w_up, w_down, layer):
    src_row, dst_row, tile_expert, tile_first, tile_valid, n_used = plan
    n_tiles = tile_expert.shape[0]
    tm = EXPERT_TILE
    d = w_gate.shape[-2]
    de = w_gate.shape[-1]
    assert d // LANES == ROW_BLOCK
    t_all = h_all.shape[0] // ROW_BLOCK
    src = (src_row * ROW_BLOCK).reshape(n_tiles, 1, tm)
    dst = (dst_row * ROW_BLOCK).reshape(n_tiles, 1, tm)

    def tile(i, nu):
        return jnp.clip(i, 0, nu[0] - 1)

    def w_map(i, te, tf, nv, nu):
        return (layer, te[tile(i, nu)], 0, 0)

    def idx_spec(shift):
        return pl.BlockSpec((None, 1, tm), lambda i, te, tf, nv, nu: (tile(i + shift, nu), 0, 0),
                            memory_space=pltpu.SMEM)

    grid_spec = pltpu.PrefetchScalarGridSpec(
        num_scalar_prefetch=4,
        grid=(n_tiles,),
        in_specs=[idx_spec(0), idx_spec(1), idx_spec(-1), idx_spec(0),
                  pl.BlockSpec(memory_space=pl.ANY),
                  pl.BlockSpec((None, None, d, de), w_map),
                  pl.BlockSpec((None, None, d, de), w_map),
                  pl.BlockSpec((None, None, de, d), w_map)],
        out_specs=pl.BlockSpec(memory_space=pl.ANY),
        scratch_shapes=[pltpu.VMEM((2, tm * ROW_BLOCK, LANES), F32), pltpu.VMEM((3, tm * ROW_BLOCK, LANES), F32),
                        pltpu.VMEM((tm, d), BF16),
                        pltpu.SemaphoreType.DMA((2,)), pltpu.SemaphoreType.DMA((3,)),
                        pltpu.VMEM((d, de), BF16), pltpu.VMEM((d, de), BF16), pltpu.VMEM((de, d), BF16)],
    )
    return pl.pallas_call(
        _expert_kernel,
        grid_spec=grid_spec,
        out_shape=jax.ShapeDtypeStruct((TOP_K * t_all * ROW_BLOCK, LANES), F32),
        compiler_params=_cparams(1, disable_bounds_checks=True),
        name="routed_experts",
    )(tile_expert, tile_first, tile_valid, n_used, src, src, dst, dst, h_all, w_gate, w_up, w_down)


def _combine_kernel(x_ref, h_ref, wgt_ref, gate_ref, sg_ref, su_ref, sd_ref, y_ref, o_ref):
    tc, d = x_ref.shape
    hb = h_ref[...]
    a = (_silu(_dot(hb, sg_ref[...])) * _dot(hb, su_ref[...])).astype(BF16)
    shared = _dot(a, sd_ref[...])
    wgt = wgt_ref[...]
    routed = None
    for k in range(TOP_K):
        term = [wgt[:, k:k + 1] * chunk for chunk in _from_row_major_tiles(y_ref, (k,), tc, d)]
        routed = term if routed is None else [r + t for r, t in zip(routed, term)]
    moe = shared + jnp.concatenate(routed, axis=1)
    o_ref[...] = x_ref[...] + gate_ref[...] * moe


def _combine_call(x, h, wgt, gate, ws_gate, ws_up, ws_down, y, row0, rows_per_mod, tc=128):
    m, d = x.shape
    tc = min(tc, m, rows_per_mod)
    assert row0 % tc == 0
    ds = ws_gate.shape[1]
    return pl.pallas_call(
        _combine_kernel,
        grid=(m // tc,),
        in_specs=[pl.BlockSpec((tc, d), lambda i: (i, 0)),
                  pl.BlockSpec((tc, d), lambda i: (i, 0)),
                  pl.BlockSpec((tc, TOP_K), lambda i: (i, 0)),
                  _mod_spec(gate, tc, rows_per_mod),
                  pl.BlockSpec((d, ds), lambda i: (0, 0)),
                  pl.BlockSpec((d, ds), lambda i: (0, 0)),
                  pl.BlockSpec((ds, d), lambda i: (0, 0)),
                  pl.BlockSpec((TOP_K, tc * ROW_BLOCK, LANES), lambda i: (0, row0 // tc + i, 0))],
        out_specs=pl.BlockSpec((tc, d), lambda i: (i, 0)),
        out_shape=jax.ShapeDtypeStruct((m, d), F32),
        compiler_params=_cparams(1),
        name="moe_combine_residual",
    )(x, h, wgt, gate, ws_gate, ws_up, ws_down, y)


def _table_lookup(table, idx):
    n = table.shape[0]
    return jnp.sum(jnp.where(idx[..., None] == jnp.arange(n, dtype=I32), table, 0), axis=-1)


def _dispatch_plan(eidx, counts):
    t, kk = eidx.shape
    tm = EXPERT_TILE
    n_pairs = t * kk
    n_tiles = (n_pairs + N_EXPERTS * (tm - 1)) // tm
    gsz = (counts + tm - 1) // tm * tm
    gend = jnp.cumsum(gsz)
    gstart = gend - gsz
    cstart = jnp.cumsum(counts) - counts
    bits = (n_pairs - 1).bit_length()
    pair = jnp.arange(n_pairs, dtype=I32).reshape(t, kk)
    order = jnp.sort(((eidx << bits) | pair).reshape(n_pairs)) & ((1 << bits) - 1)
    tile_start = jnp.arange(n_tiles, dtype=I32) * tm
    tile_expert = jnp.minimum(jnp.sum((gend[None, :] <= tile_start[:, None]).astype(I32), axis=1),
                              N_EXPERTS - 1)
    first = _table_lookup(cstart, tile_expert)
    cnt = _table_lookup(counts, tile_expert)
    last = first + jnp.maximum(cnt - 1, 0)
    off = tile_start - _table_lookup(gstart, tile_expert)
    j = jnp.clip((first + off)[:, None] + jnp.arange(tm, dtype=I32)[None, :], first[:, None], last[:, None])
    pair_of_slot = order[jnp.clip(j, 0, n_pairs - 1)]
    src_row = pair_of_slot // kk
    dst_row = (pair_of_slot % kk) * t + src_row
    tile_valid = jnp.clip(cnt - off, 0, tm)
    n_used = (gend[-1] // tm).astype(I32).reshape(1)
    prev = jnp.concatenate([jnp.full((1,), -1, I32), tile_expert[:-1]])
    tile_first = (tile_expert != prev).astype(I32)
    return (src_row.astype(I32), dst_row.astype(I32), tile_expert.astype(I32), tile_first,
            tile_valid.astype(I32), n_used)


def _moe_layer(xp, xs, mp, ms, g_ffn, w_router, b_router, w_gate, w_up, w_down, ws_gate, ws_up, ws_down, seq,
               layer):
    bd = xs.shape[0]
    cnt0 = jnp.zeros((1, N_EXPERTS), F32)
    htp, hp, ep, wp, cnt_p = _router_call(xp, g_ffn, mp[4], mp[3], w_router, b_router, cnt0, seq)
    hts, hs, es, ws, cnt = _router_call(xs, g_ffn, ms[4], ms[3], w_router, b_router, cnt_p, bd)
    h_all = jnp.concatenate([htp, hts], axis=0)
    t_all = xp.shape[0] + bd
    plan = _dispatch_plan(jnp.concatenate([ep, es], axis=0), cnt.reshape(N_EXPERTS).astype(I32))
    y = _expert_call(h_all, plan, w_gate, w_up, w_down, layer).reshape(TOP_K, t_all * ROW_BLOCK, LANES)
    tp = xp.shape[0]
    shared = (ws_gate.astype(BF16), ws_up.astype(BF16), ws_down.astype(BF16))
    xp = _combine_call(xp, hp, wp, mp[5], *shared, y, 0, seq)
    xs = _combine_call(xs, hs, ws, ms[5], *shared, y, tp, bd)
    return xp, xs


def kernel(x_prompt, x_sample, cache_a_k, cache_a_v, cache_b_k, cache_b_v, cache_b_idx, page_table,
           c_prompt, c_sample, w_ada, b_ada, g_mix, g_ffn, a_w_in, a_q_norm, a_k_norm, a_lambda_q1,
           a_lambda_k1, a_lambda_q2, a_lambda_k2, a_subln, a_w_out, b_w_in, b_q_norm, b_k_norm, b_w_out,
           w_router, b_router, w_gate, w_up, w_down, ws_gate, ws_up, ws_down):
    batch, seq, d = x_prompt.shape
    bd, ts, _ = x_sample.shape
    assert ts == 1
    depth = w_ada.shape[0]
    page = cache_a_k.shape[2]
    past = page_table.shape[1] * page
    tp = batch * seq

    cos_p, sin_p = _rope_tables(jnp.arange(seq, dtype=jnp.int32), LANES)
    cos_s, sin_s = _rope_tables(past + jnp.arange(ts, dtype=jnp.int32), LANES)

    n_c = batch + bd
    pad = (-n_c) % SUBLANES
    c_all = jnp.concatenate([c_prompt, c_sample, jnp.zeros((pad, d), F32)], axis=0)
    mod = _mod_call(c_all, w_ada, b_ada).reshape(depth, n_c + pad, 6, d)

    xp = x_prompt.reshape(tp, d)
    xs = x_sample.reshape(bd, d)
    outs_a = [[], [], [], []]
    outs_b = [[], [], [], [], [], []]
    for i in range(depth):
        mp = [mod[i, :batch, t].reshape(batch, 1, d) for t in range(6)]
        ms = [mod[i, batch:n_c, t].reshape(1, bd, d) for t in range(6)]
        hp = _normmod_call(xp, g_mix[i], mp[1], mp[0], seq, min(512, seq))
        hs = _normmod_call(xs, g_mix[i], ms[1], ms[0], bd, bd, out_dtype=F32)
        j = i // 2
        if i % 2 == 0:
            lam_init = 0.8 - 0.6 * math.exp(-0.3 * i)
            lams = (a_lambda_q1[j], a_lambda_k1[j], a_lambda_q2[j], a_lambda_k2[j])
            w_in = a_w_in[j]
            mix = A_HEADS * 2 * A_HEAD_DIM
            qp = _proj_call(hp, w_in, 0, mix, ("norm_rope",), gain=a_q_norm[j], rope=(cos_p, sin_p),
                            rope_rows=seq, out_f32=False, out_bf16=True,
                            out_scale=LOG2E * A_HEAD_DIM ** -0.5)
            kp, kp_b = _proj_call(hp, w_in, mix, mix, ("norm_rope",), gain=a_k_norm[j], rope=(cos_p, sin_p),
                                  rope_rows=seq, out_bf16=True)
            vp, vp_b = _proj_call(hp, w_in, 2 * mix, mix, ("plain",), out_bf16=True)
            qs = _proj_call(hs, w_in, 0, mix, ("norm_rope",), gain=a_q_norm[j], rope=(cos_s, sin_s), rope_rows=1)
            ks = _proj_call(hs, w_in, mix, mix, ("norm_rope",), gain=a_k_norm[j], rope=(cos_s, sin_s), rope_rows=1)
            vs = _proj_call(hs, w_in, 2 * mix, mix, ("plain",))
            op = _diff_flash_call(qp, kp_b, vp_b, lams, a_subln[j], batch, seq, lam_init)
            hd = 2 * A_HEAD_DIM
            os_ = _diff_decode_call(qs.reshape(bd, A_HEADS, hd), ks.reshape(bd, A_HEADS, hd),
                                    vs.reshape(bd, A_HEADS, hd), cache_a_k[j], cache_a_v[j], page_table,
                                    lams, a_subln[j], lam_init).reshape(bd, mix)
            w_out = a_w_out[j]
            outs_a[0].append(kp.reshape(batch, seq, A_HEADS, hd))
            outs_a[1].append(vp.reshape(batch, seq, A_HEADS, hd))
            outs_a[2].append(ks.reshape(bd, ts, A_HEADS, hd))
            outs_a[3].append(vs.reshape(bd, ts, A_HEADS, hd))
        else:
            w_in = b_w_in[j]
            mix = B_HEADS * B_HEAD_DIM
            dh = B_HEAD_DIM
            o_kv = mix
            o_qi = mix + 2 * dh
            o_ki = o_qi + IDX_HEADS * IDX_DIM
            o_wi = o_ki + IDX_DIM
            w_wi = w_in[:, o_wi:o_wi + IDX_HEADS]
            wi_scale = IDX_HEADS ** -0.5

            def project(h, rope, rope_rows, bf16):
                q = _proj_call(h, w_in, 0, mix, ("norm_rope",), gain=b_q_norm[j], rope=rope, rope_rows=rope_rows,
                               out_f32=not bf16, out_bf16=bf16)
                kv = _proj_call(h, w_in, o_kv, 2 * dh, ("norm_rope", "plain"), gain=b_k_norm[j], rope=rope,
                                rope_rows=rope_rows, out_bf16=bf16, tn=2 * dh)
                qi = _proj_call(h, w_in, o_qi, IDX_HEADS * IDX_DIM, ("rope",), rope=rope, rope_rows=rope_rows,
                                out_f32=not bf16, out_bf16=bf16, tm=1024, tn=256)
                ki = _proj_call(h, w_in, o_ki, IDX_DIM, ("rope",), rope=rope, rope_rows=rope_rows,
                                out_bf16=bf16, tn=IDX_DIM)
                wi = _proj_call(h, w_wi, 0, IDX_HEADS, ("plain",), out_scale=wi_scale, tn=IDX_HEADS)
                return q, kv, qi, ki, wi

            qp, (kvp, kvp_b), qip, (kip, kip_b), wip = project(hp, (cos_p, sin_p), seq, True)
            qs, kvs, qis, kis, wis = project(hs, (cos_s, sin_s), 1, False)
            n_sel = min(IDX_TOPK_MAX, seq // 4)
            sc_p = _index_prompt_call(qip, wip, kip_b, batch, seq)
            thr_p = _kth_call(sc_p, n_sel, causal_seq=seq)
            op = _dsa_prompt_call(qp, kvp_b, sc_p, thr_p, batch, seq)
            sc_s, sc_new = _index_sample_call(qis.reshape(bd, IDX_HEADS, IDX_DIM), wis.reshape(bd, IDX_HEADS, 1),
                                              kis.reshape(bd, 1, IDX_DIM), cache_b_idx[j], page_table)
            n_sel_s = min(IDX_TOPK_MAX, (past + ts) // 4)
            thr_s = _kth_call(jnp.concatenate([sc_s.reshape(bd, past), sc_new.reshape(bd, LANES)], axis=1),
                              n_sel_s, tr=bd)
            os_ = _dsa_sample_call(qs.reshape(bd, B_HEADS, dh), kvs[:, :dh].reshape(bd, 1, dh),
                                   kvs[:, dh:].reshape(bd, 1, dh), sc_s, sc_new, thr_s.reshape(bd, 1, 1),
                                   cache_b_k[j], cache_b_v[j], page_table).reshape(bd, mix)
            w_out = b_w_out[j]
            outs_b[0].append(kvp[:, :dh].reshape(batch, seq, dh))
            outs_b[1].append(kvp[:, dh:].reshape(batch, seq, dh))
            outs_b[2].append(kip.reshape(batch, seq, IDX_DIM))
            outs_b[3].append(kvs[:, :dh].reshape(bd, ts, dh))
            outs_b[4].append(kvs[:, dh:].reshape(bd, ts, dh))
            outs_b[5].append(kis.reshape(bd, ts, IDX_DIM))
        xp = _resid_call(op, w_out, xp, mp[2], seq)
        xs = _resid_call(os_, w_out, xs, ms[2], bd)
        xp, xs = _moe_layer(xp, xs, mp, ms, g_ffn[i], w_router[i], b_router[i], w_gate, w_up, w_down,
                            ws_gate[i], ws_up[i], ws_down[i], seq, i)
    return (xp.reshape(batch, seq, d), xs.reshape(bd, ts, d),
            *[jnp.stack(o) for o in outs_a], *[jnp.stack(o) for o in outs_b])
```
